```python
import jax, jax.numpy as jnp
from jax import lax
import numpy as np

D_MODEL = 2048
BATCH = 8
SEQ = 2048
DEPTH = 4
DEC_BATCH = 8
DEC_SEQ = 32
PAST_LEN = 2048

CHUNK = 64
N_MIXERS = 3
EPS = 1e-6
D_FF = 4 * D_MODEL
CONV_WIDTH = 31
CONV_STATE = CONV_WIDTH - 1
FOX_HEADS = 16
FOX_HEAD_DIM = D_MODEL // FOX_HEADS
Q_BLOCK = 128
GLA_HEADS = 4
GLA_KEY_DIM = D_MODEL // 2 // GLA_HEADS
GLA_VAL_DIM = D_MODEL // GLA_HEADS
GLA_QK_WIDTH = GLA_HEADS * GLA_KEY_DIM
GLA_GATE_RANK = 16
GLA_TAU = 16.0

kernel_name = "hybrid_streaming_encoder_step"


def rms_norm(x, g):
    xf = x.astype(jnp.float32)
    y = xf * lax.rsqrt(jnp.mean(jnp.square(xf), axis=-1, keepdims=True) + EPS)
    return (y * g.astype(jnp.float32)).astype(x.dtype)


def sqrelu_mlp(h, w_up, w_down):
    return jnp.square(jax.nn.relu(h @ w_up)) @ w_down


def conv_module(h, hist, w_in, w_dw, g_norm, w_out):
    a, b = jnp.split(h @ w_in, 2, axis=-1)
    u = a * jax.nn.sigmoid(b)
    full = jnp.concatenate([hist.astype(u.dtype), u], axis=1)
    y = lax.conv_general_dilated(full, w_dw.astype(u.dtype)[:, None, :], (1,), 'VALID',
                                 dimension_numbers=('NWC', 'WIO', 'NWC'),
                                 feature_group_count=D_MODEL)
    y = jax.nn.silu(rms_norm(y, g_norm))
    return y @ w_out, full[:, -CONV_STATE:]


def conv_mixer(hp, hs, cache_conv, w_in, w_dw, g_norm, w_out):
    zeros = jnp.zeros((hp.shape[0], CONV_STATE, D_MODEL), hp.dtype)
    yp, sp = conv_module(hp, zeros, w_in, w_dw, g_norm, w_out)
    ys, ss = conv_module(hs, cache_conv, w_in, w_dw, g_norm, w_out)
    return yp, ys, (sp, ss.astype(cache_conv.dtype))


def fox_project(h, w_qkv, w_f, b_f, g_q, g_k):
    B, T, _ = h.shape
    q, k, v = jnp.split(h @ w_qkv, 3, axis=-1)
    q = rms_norm(q.reshape(B, T, FOX_HEADS, FOX_HEAD_DIM), g_q)
    k = rms_norm(k.reshape(B, T, FOX_HEADS, FOX_HEAD_DIM), g_k)
    v = v.reshape(B, T, FOX_HEADS, FOX_HEAD_DIM)
    logf = jax.nn.log_sigmoid((h @ w_f + b_f).astype(jnp.float32))
    return q, k, v, logf


def fox_attend(q, c_q, pos_q, k, v, c_k, pos_k):
    s = jnp.einsum('bqhd,bkhd->bhqk', q, k).astype(jnp.float32) * (FOX_HEAD_DIM ** -0.5)
    s = s + jnp.transpose(c_q, (0, 2, 1))[:, :, :, None] - jnp.transpose(c_k, (0, 2, 1))[:, :, None, :]
    s = jnp.where(pos_k[None, :] <= pos_q[:, None], s, -jnp.inf)
    p = jax.nn.softmax(s, axis=-1)
    return jnp.einsum('bhqk,bkhd->bqhd', p.astype(v.dtype), v)


def fox_mixer(hp, hs, cache_k, cache_v, cache_logf, w_qkv, w_f, b_f, g_q, g_k, w_o):
    B, T, _ = hp.shape
    qp, kp, vp, lfp = fox_project(hp, w_qkv, w_f, b_f, g_q, g_k)
    cp = jnp.cumsum(lfp, axis=1)
    pos = jnp.arange(T, dtype=jnp.int32)
    nb = T // Q_BLOCK
    q_blocks = jnp.swapaxes(qp.reshape(B, nb, Q_BLOCK, FOX_HEADS, FOX_HEAD_DIM), 0, 1)
    c_blocks = jnp.swapaxes(cp.reshape(B, nb, Q_BLOCK, FOX_HEADS), 0, 1)
    p_blocks = pos.reshape(nb, Q_BLOCK)
    op = lax.map(lambda a: fox_attend(a[0], a[1], a[2], kp, vp, cp, pos),
                 (q_blocks, c_blocks, p_blocks))
    op = jnp.swapaxes(op, 0, 1).reshape(B, T, D_MODEL)
    Bs, Ts, _ = hs.shape
    P = cache_k.shape[1]
    qs, ks, vs, lfs = fox_project(hs, w_qkv, w_f, b_f, g_q, g_k)
    c_past = jnp.cumsum(cache_logf.astype(jnp.float32), axis=1)
    c_past = c_past - c_past[:, -1:]
    c_new = jnp.cumsum(lfs, axis=1)
    k_all = jnp.concatenate([cache_k.astype(ks.dtype), ks], axis=1)
    v_all = jnp.concatenate([cache_v.astype(vs.dtype), vs], axis=1)
    c_all = jnp.concatenate([c_past, c_new], axis=1)
    pos_q = P + jnp.arange(Ts, dtype=jnp.int32)
    pos_k = jnp.arange(P + Ts, dtype=jnp.int32)
    os_ = fox_attend(qs, c_new, pos_q, k_all, v_all, c_all, pos_k).reshape(Bs, Ts, D_MODEL)
    new = (kp, vp, lfp.astype(hp.dtype),
           ks.astype(cache_k.dtype), vs.astype(cache_v.dtype), lfs.astype(cache_logf.dtype))
    return op @ w_o, os_ @ w_o, new


def gla_project(h, w_qkvr, w_a1, w_a2, b_a):
    B, T, _ = h.shape
    q, k, v, r = jnp.split(h @ w_qkvr, [GLA_QK_WIDTH, 2 * GLA_QK_WIDTH, 2 * GLA_QK_WIDTH + D_MODEL], axis=-1)
    q = q.reshape(B, T, GLA_HEADS, GLA_KEY_DIM).astype(jnp.float32) * (GLA_KEY_DIM ** -0.5)
    k = k.reshape(B, T, GLA_HEADS, GLA_KEY_DIM).astype(jnp.float32)
    v = v.reshape(B, T, GLA_HEADS, GLA_VAL_DIM).astype(jnp.float32)
    loga = jax.nn.log_sigmoid(((h @ w_a1) @ w_a2 + b_a).astype(jnp.float32)) / GLA_TAU
    loga = loga.reshape(B, T, GLA_HEADS, GLA_KEY_DIM)
    return q, k, v, r, loga


def gla_block(S, q, k, v, loga):
    L = q.shape[1]
    b = jnp.cumsum(loga, axis=1)
    causal = jnp.tril(jnp.ones((L, L), dtype=bool))
    diff = b[:, :, None] - b[:, None, :]
    decay = jnp.exp(jnp.where(causal[None, :, :, None, None], diff, -jnp.inf))
    A = jnp.einsum('bthd,bshd,btshd->bhts', q, k, decay)
    o = jnp.einsum('bhts,bshv->bthv', A, v) + jnp.einsum('bthd,bhdv->bthv', q * jnp.exp(b), S)
    b_last = b[:, -1]
    S_new = jnp.exp(b_last)[..., None] * S + jnp.einsum('bshd,bshv->bhdv', k * jnp.exp(b_last[:, None] - b), v)
    return S_new, o


def gla_output(o, r, g_o, w_o, dtype):
    B, T = o.shape[:2]
    o = rms_norm(o.astype(dtype), g_o) * jax.nn.silu(r.reshape(B, T, GLA_HEADS, GLA_VAL_DIM))
    return o.reshape(B, T, D_MODEL) @ w_o


def gla_mixer(hp, hs, state_gla, w_qkvr, w_a1, w_a2, b_a, g_o, w_o):
    B, T, _ = hp.shape
    nc = T // CHUNK
    q, k, v, r, la = gla_project(hp, w_qkvr, w_a1, w_a2, b_a)
    to_blocks = lambda a: jnp.swapaxes(a.reshape(B, nc, CHUNK, *a.shape[2:]), 0, 1)
    S0 = jnp.zeros((B, GLA_HEADS, GLA_KEY_DIM, GLA_VAL_DIM), jnp.float32)
    Sp, o = lax.scan(lambda S, a: gla_block(S, *a), S0,
                     (to_blocks(q), to_blocks(k), to_blocks(v), to_blocks(la)))
    o = jnp.swapaxes(o, 0, 1).reshape(B, T, GLA_HEADS, GLA_VAL_DIM)
    yp = gla_output(o, r, g_o, w_o, hp.dtype)
    qs, ks, vs, rs, las = gla_project(hs, w_qkvr, w_a1, w_a2, b_a)
    Ss, os_ = gla_block(state_gla.astype(jnp.float32), qs, ks, vs, las)
    ys = gla_output(os_, rs, g_o, w_o, hs.dtype)
    return yp, ys, (Sp.astype(hp.dtype), Ss.astype(state_gla.dtype))


def setup_inputs(seed: int = 0) -> dict:
    key = jax.random.key(seed)
    keys = iter(jax.random.split(key, 96))
    f32 = jnp.float32
    nrm = lambda shape, scale: jax.random.normal(next(keys), shape, f32) * scale
    gain = lambda n: 1.0 + 0.05 * jax.random.normal(next(keys), (n,), f32)
    D = D_MODEL
    inp = {}
    inp['x_prompt'] = nrm((BATCH, SEQ, D), 1.0)
    inp['x_sample'] = nrm((DEC_BATCH, DEC_SEQ, D), 1.0)
    inp['cache_conv_l0'] = nrm((DEC_BATCH, CONV_STATE, D), 0.5)
    inp['cache_k_l1'] = nrm((DEC_BATCH, PAST_LEN, FOX_HEADS, FOX_HEAD_DIM), 1.0)
    inp['cache_v_l1'] = nrm((DEC_BATCH, PAST_LEN, FOX_HEADS, FOX_HEAD_DIM), 1.0)
    inp['cache_logf_l1'] = jax.nn.log_sigmoid(3.0 + nrm((DEC_BATCH, PAST_LEN, FOX_HEADS), 1.0))
    inp['state_gla_l2'] = nrm((DEC_BATCH, GLA_HEADS, GLA_KEY_DIM, GLA_VAL_DIM), 2.0)
    inp['cache_conv_l3'] = nrm((DEC_BATCH, CONV_STATE, D), 0.5)

    def ffn(l):
        inp[f'norm_ffn_l{l}'] = gain(D)
        inp[f'ffn_w_up_l{l}'] = nrm((D, D_FF), D ** -0.5)
        inp[f'ffn_w_down_l{l}'] = nrm((D_FF, D), D_FF ** -0.5)

    def conv(l):
        inp[f'norm_mix_l{l}'] = gain(D)
        inp[f'conv_w_in_l{l}'] = nrm((D, 2 * D), D ** -0.5)
        inp[f'conv_w_dw_l{l}'] = nrm((CONV_WIDTH, D), CONV_WIDTH ** -0.5)
        inp[f'conv_norm_l{l}'] = gain(D)
        inp[f'conv_w_out_l{l}'] = nrm((D, D), D ** -0.5)
        ffn(l)

    conv(0)
    inp['norm_mix_l1'] = gain(D)
    inp['fox_w_qkv_l1'] = nrm((D, 3 * D), D ** -0.5)
    inp['fox_w_f_l1'] = nrm((D, FOX_HEADS), D ** -0.5)
    inp['fox_b_f_l1'] = 3.0 + nrm((FOX_HEADS,), 0.5)
    inp['fox_q_norm_l1'] = gain(FOX_HEAD_DIM)
    inp['fox_k_norm_l1'] = gain(FOX_HEAD_DIM)
    inp['fox_w_o_l1'] = nrm((D, D), D ** -0.5)
    ffn(1)
    inp['norm_mix_l2'] = gain(D)
    inp['gla_w_qkvr_l2'] = nrm((D, 2 * GLA_QK_WIDTH + 2 * D), D ** -0.5)
    inp['gla_w_a1_l2'] = nrm((D, GLA_GATE_RANK), D ** -0.5)
    inp['gla_w_a2_l2'] = nrm((GLA_GATE_RANK, GLA_QK_WIDTH), GLA_GATE_RANK ** -0.5)
    inp['gla_b_a_l2'] = nrm((GLA_QK_WIDTH,), 0.1)
    inp['gla_o_norm_l2'] = gain(GLA_VAL_DIM)
    inp['gla_w_o_l2'] = nrm((D, D), D ** -0.5)
    ffn(2)
    conv(3)
    return inp


def reference(x_prompt, x_sample, cache_conv_l0, cache_k_l1, cache_v_l1, cache_logf_l1, state_gla_l2, cache_conv_l3,
              norm_mix_l0, conv_w_in_l0, conv_w_dw_l0, conv_norm_l0, conv_w_out_l0, norm_ffn_l0, ffn_w_up_l0, ffn_w_down_l0,
              norm_mix_l1, fox_w_qkv_l1, fox_w_f_l1, fox_b_f_l1, fox_q_norm_l1, fox_k_norm_l1, fox_w_o_l1, norm_ffn_l1, ffn_w_up_l1, ffn_w_down_l1,
              norm_mix_l2, gla_w_qkvr_l2, gla_w_a1_l2, gla_w_a2_l2, gla_b_a_l2, gla_o_norm_l2, gla_w_o_l2, norm_ffn_l2, ffn_w_up_l2, ffn_w_down_l2,
              norm_mix_l3, conv_w_in_l3, conv_w_dw_l3, conv_norm_l3, conv_w_out_l3, norm_ffn_l3, ffn_w_up_l3, ffn_w_down_l3):
    norm_mix = [norm_mix_l0, norm_mix_l1, norm_mix_l2, norm_mix_l3]
    mix_args = [
        (cache_conv_l0, conv_w_in_l0, conv_w_dw_l0, conv_norm_l0, conv_w_out_l0),
        (cache_k_l1, cache_v_l1, cache_logf_l1, fox_w_qkv_l1, fox_w_f_l1, fox_b_f_l1, fox_q_norm_l1, fox_k_norm_l1, fox_w_o_l1),
        (state_gla_l2, gla_w_qkvr_l2, gla_w_a1_l2, gla_w_a2_l2, gla_b_a_l2, gla_o_norm_l2, gla_w_o_l2),
        (cache_conv_l3, conv_w_in_l3, conv_w_dw_l3, conv_norm_l3, conv_w_out_l3),
    ]
    ffn_args = [(norm_ffn_l0, ffn_w_up_l0, ffn_w_down_l0), (norm_ffn_l1, ffn_w_up_l1, ffn_w_down_l1),
                (norm_ffn_l2, ffn_w_up_l2, ffn_w_down_l2), (norm_ffn_l3, ffn_w_up_l3, ffn_w_down_l3)]
    mixers = [conv_mixer, fox_mixer, gla_mixer]
    xp, xs = x_prompt, x_sample
    new_state = []
    for i in range(DEPTH):
        mixer = mixers[i % N_MIXERS]
        dp, ds, st = mixer(rms_norm(xp, norm_mix[i]), rms_norm(xs, norm_mix[i]), *mix_args[i])
        xp = xp + dp
        xs = xs + ds
        new_state.extend(st)
        g_f, w_up, w_down = ffn_args[i]
        xp = xp + sqrelu_mlp(rms_norm(xp, g_f), w_up, w_down)
        xs = xs + sqrelu_mlp(rms_norm(xs, g_f), w_up, w_down)
    y_prompt, y_sample = xp, xs
    return (y_prompt, y_sample, *new_state)
```

```python
import functools

import jax
import jax.numpy as jnp
from jax import lax
from jax.experimental import pallas as pl
from jax.experimental.pallas import tpu as pltpu

F32 = jnp.float32
BF16 = jnp.bfloat16

EPS = 1e-6
GLA_CHUNK = 64
GLA_SUB = 16
GLA_TAU = 16.0
LANES = 128
CONV_HALO = 32
VMEM_LIMIT_BYTES = 56 * 1024 * 1024


def _tile(n, pref):
    if n <= pref:
        return n
    t = pref - pref % 8
    while t >= 8:
        if n % t == 0:
            return t
        t -= 8
    return n


def _params(*sem):
    return pltpu.CompilerParams(dimension_semantics=sem, vmem_limit_bytes=VMEM_LIMIT_BYTES)


def _log_sigmoid(x):
    return jnp.minimum(x, 0.0) - jnp.log1p(jnp.exp(-jnp.abs(x)))


def _dot(a, b):
    return jnp.dot(a, b, preferred_element_type=F32)


def _dot_nt(a, b):
    return lax.dot_general(a, b, (((1,), (1,)), ((), ())), preferred_element_type=F32)


def _dot_tn(a, b):
    return lax.dot_general(a, b, (((0,), (0,)), ((), ())), preferred_element_type=F32)


def _rmsnorm_kernel(x_ref, g_ref, o_ref):
    x = x_ref[...]
    ms = jnp.mean(x * x, axis=-1, keepdims=True)
    o_ref[...] = (x * lax.rsqrt(ms + EPS) * g_ref[...]).astype(o_ref.dtype)


def _rmsnorm(x, g):
    m, d = x.shape
    tm = _tile(m, 512)
    return pl.pallas_call(
        _rmsnorm_kernel,
        grid=(m // tm,),
        in_specs=[pl.BlockSpec((tm, d), lambda i: (i, 0)), pl.BlockSpec((1, d), lambda i: (0, 0))],
        out_specs=pl.BlockSpec((tm, d), lambda i: (i, 0)),
        out_shape=jax.ShapeDtypeStruct((m, d), BF16),
        compiler_params=_params("parallel"),
        name="rmsnorm",
    )(x, g.reshape(1, d))


def _mm_plain_kernel(a_ref, w_ref, *o_refs):
    acc = _dot(a_ref[...], w_ref[...])
    for o_ref in o_refs:
        o_ref[...] = acc.astype(o_ref.dtype)


def _mm_headnorm_kernel(a_ref, w_ref, g_ref, *o_refs, heads_per_tile, head_dim):
    acc = _dot(a_ref[...], w_ref[...])
    g = g_ref[...]
    for h in range(heads_per_tile):
        sl = slice(h * head_dim, (h + 1) * head_dim)
        x = acc[:, sl]
        ms = jnp.mean(x * x, axis=-1, keepdims=True)
        y = x * lax.rsqrt(ms + EPS) * g
        for o_ref in o_refs:
            o_ref[:, sl] = y.astype(o_ref.dtype)


def _mm_glu_kernel(a_ref, wa_ref, wb_ref, o_ref):
    a = a_ref[...]
    o_ref[...] = _dot(a, wa_ref[...]) * jax.nn.sigmoid(_dot(a, wb_ref[...]))


def _mm_res_kernel(a_ref, w_ref, r_ref, o_ref):
    o_ref[...] = r_ref[...] + _dot(a_ref[...], w_ref[...])


def _mm_logsig_kernel(a_ref, w_ref, b_ref, o_ref):
    o_ref[...] = _log_sigmoid(_dot(a_ref[...], w_ref[...]) + b_ref[...])


def _gate_kernel(a_ref, w1_ref, w2_ref, b_ref, o_ref):
    low = _dot(a_ref[...], w1_ref[...]).astype(BF16)
    o_ref[...] = _log_sigmoid(_dot(low, w2_ref[...]) + b_ref[...]) * (1.0 / GLA_TAU)


def _mm(kernel, a, weights, extras, out_dtypes, n, *, tm_pref=1024, tn_pref=512):
    m, k = a.shape
    tm = _tile(m, tm_pref)
    tn = _tile(n, tn_pref)
    in_specs = [pl.BlockSpec((tm, k), lambda i, j: (i, 0))]
    args = [a]
    for w, off in weights:
        ob = off // tn
        assert off % tn == 0
        in_specs.append(pl.BlockSpec((k, tn), lambda i, j, ob=ob: (0, j + ob)))
        args.append(w)
    for arr, kind in extras:
        if kind == "row":
            in_specs.append(pl.BlockSpec((1, tn), lambda i, j: (0, j)))
        elif kind == "const":
            in_specs.append(pl.BlockSpec(arr.shape, lambda i, j, nd=arr.ndim: (0,) * nd))
        else:
            in_specs.append(pl.BlockSpec((tm, tn), lambda i, j: (i, j)))
        args.append(arr)
    outs = pl.pallas_call(
        kernel,
        grid=(m // tm, n // tn),
        in_specs=in_specs,
        out_specs=[pl.BlockSpec((tm, tn), lambda i, j: (i, j)) for _ in out_dtypes],
        out_shape=[jax.ShapeDtypeStruct((m, n), dt) for dt in out_dtypes],
        compiler_params=_params("parallel", "arbitrary"),
        name=getattr(kernel, "__name__", None) or kernel.func.__name__,
    )(*args)
    return outs


def _gate(h, w1p, w2p, b):
    m, k = h.shape
    n = w2p.shape[1]
    tm = _tile(m, 512)
    return pl.pallas_call(
        _gate_kernel,
        grid=(m // tm,),
        in_specs=[pl.BlockSpec((tm, k), lambda i: (i, 0)),
                  pl.BlockSpec(w1p.shape, lambda i: (0, 0)),
                  pl.BlockSpec(w2p.shape, lambda i: (0, 0)),
                  pl.BlockSpec((1, n), lambda i: (0, 0))],
        out_specs=pl.BlockSpec((tm, n), lambda i: (i, 0)),
        out_shape=jax.ShapeDtypeStruct((m, n), F32),
        compiler_params=_params("parallel"),
        name="gla_gate",
    )(h, w1p, w2p, b.reshape(1, n))


def _ffn_kernel(x_ref, g_ref, wu_ref, wd_ref, o_ref, h_scr, acc_scr, *, nj):
    j = pl.program_id(1)

    @pl.when(j == 0)
    def _():
        x = x_ref[...]
        ms = jnp.mean(x * x, axis=-1, keepdims=True)
        h_scr[...] = (x * lax.rsqrt(ms + EPS) * g_ref[...]).astype(BF16)
        acc_scr[...] = jnp.zeros_like(acc_scr)

    u = jnp.maximum(_dot(h_scr[...], wu_ref[...]), 0.0)
    acc_scr[...] += _dot((u * u).astype(BF16), wd_ref[...])

    @pl.when(j == nj - 1)
    def _():
        o_ref[...] = x_ref[...] + acc_scr[...]


def _ffn(x, g, wu, wd, *, tm_pref=512, tf_pref=512):
    m, d = x.shape
    f = wu.shape[1]
    tm = _tile(m, tm_pref)
    tf = _tile(f, tf_pref)
    nj = f // tf
    return pl.pallas_call(
        functools.partial(_ffn_kernel, nj=nj),
        grid=(m // tm, nj),
        in_specs=[pl.BlockSpec((tm, d), lambda i, j: (i, 0)),
                  pl.BlockSpec((1, d), lambda i, j: (0, 0)),
                  pl.BlockSpec((d, tf), lambda i, j: (0, j)),
                  pl.BlockSpec((tf, d), lambda i, j: (j, 0))],
        out_specs=pl.BlockSpec((tm, d), lambda i, j: (i, 0)),
        out_shape=jax.ShapeDtypeStruct((m, d), F32),
        scratch_shapes=[pltpu.VMEM((tm, d), BF16), pltpu.VMEM((tm, d), F32)],
        compiler_params=_params("parallel", "arbitrary"),
        name="ffn",
    )(x, g.reshape(1, d), wu, wd)


def _conv_kernel(u_ref, up_ref, hist_ref, w_ref, g_ref, o_ref, ext_scr, y_scr, *, tm, width, rb, lc):
    i = pl.program_id(1)
    d = u_ref.shape[-1]

    @pl.when(i == 0)
    def _():
        ext_scr[0:CONV_HALO, :] = hist_ref[...]

    @pl.when(i > 0)
    def _():
        ext_scr[0:CONV_HALO, :] = up_ref[...]

    ext_scr[CONV_HALO:CONV_HALO + tm, :] = u_ref[...]
    off = CONV_HALO - (width - 1)

    def row_block(r, carry):
        r0 = pl.multiple_of(r * rb, rb)
        for c in range(d // lc):
            cs = slice(c * lc, (c + 1) * lc)
            acc = jnp.zeros((rb, lc), F32)
            for rem in range(8):
                part = None
                for k in range(width):
                    if (off + k) % 8 != rem:
                        continue
                    base = (off + k) // 8 * 8
                    rows = rb + 8 if rem else rb
                    term = ext_scr[pl.ds(r0 + base, rows), cs] * w_ref[k:k + 1, cs]
                    part = term if part is None else part + term
                if part is None:
                    continue
                if rem:
                    part = pltpu.roll(part, rb + 8 - rem, axis=0)
                acc = acc + part[:rb]
            y_scr[pl.ds(r0, rb), cs] = acc
        return carry

    lax.fori_loop(0, tm // rb, row_block, 0)
    y = y_scr[...]
    ms = jnp.mean(y * y, axis=-1, keepdims=True)
    z = y * lax.rsqrt(ms + EPS) * g_ref[...]
    o_ref[...] = (z * jax.nn.sigmoid(z)).astype(o_ref.dtype)


def _conv_norm_silu(u, hist_pad, w_dw, g):
    nb, nt, d = u.shape
    width = w_dw.shape[0]
    assert width - 1 <= CONV_HALO
    tm = _tile(nt, 256)
    assert tm % CONV_HALO == 0
    rb = 32 if tm % 32 == 0 else tm
    lc = min(512, d)
    hb = tm // CONV_HALO
    return pl.pallas_call(
        functools.partial(_conv_kernel, tm=tm, width=width, rb=rb, lc=lc),
        grid=(nb, nt // tm),
        in_specs=[pl.BlockSpec((None, tm, d), lambda b, i: (b, i, 0)),
                  pl.BlockSpec((None, CONV_HALO, d), lambda b, i: (b, jnp.maximum(i * hb - 1, 0), 0)),
                  pl.BlockSpec((None, CONV_HALO, d), lambda b, i: (b, 0, 0)),
                  pl.BlockSpec((width, d), lambda b, i: (0, 0)),
                  pl.BlockSpec((1, d), lambda b, i: (0, 0))],
        out_specs=pl.BlockSpec((None, tm, d), lambda b, i: (b, i, 0)),
        out_shape=jax.ShapeDtypeStruct((nb, nt, d), BF16),
        scratch_shapes=[pltpu.VMEM((CONV_HALO + tm, d), F32), pltpu.VMEM((tm, d), F32)],
        compiler_params=_params("parallel", "arbitrary"),
        name="conv_norm_silu",
    )(u, u, hist_pad, w_dw, g.reshape(1, d))


def _cumsum_kernel(x_ref, o_ref, *, n, sub_last):
    x = x_ref[...]
    lane = lax.broadcasted_iota(jnp.int32, x.shape, 1)
    s = 1
    while s < n:
        x = x + jnp.where(lane >= s, pltpu.roll(x, s, axis=1), 0.0)
        s *= 2
    if sub_last:
        x = x - x[:, n - 1:n]
    o_ref[...] = x


def _cumsum_lanes(x, *, sub_last=False):
    nb, h, n = x.shape
    return pl.pallas_call(
        functools.partial(_cumsum_kernel, n=n, sub_last=sub_last),
        grid=(nb,),
        in_specs=[pl.BlockSpec((None, h, n), lambda b: (b, 0, 0))],
        out_specs=pl.BlockSpec((None, h, n), lambda b: (b, 0, 0)),
        out_shape=jax.ShapeDtypeStruct((nb, h, n), F32),
        compiler_params=_params("parallel"),
        name="cumsum_logf",
    )(x)


def _flash_kernel(q_ref, k_ref, v_ref, c_ref, o_ref, *, tq, scale):
    qi = pl.program_id(2)
    q = q_ref[...]
    hd = q.shape[-1]

    def step(kj, carry, masked):
        m, l, acc = carry
        k0 = pl.multiple_of(kj * tq, tq)
        s = _dot_nt(q, k_ref[pl.ds(k0, tq), :]) * scale - c_ref[:, pl.ds(k0, tq)]
        if masked:
            row = lax.broadcasted_iota(jnp.int32, (tq, tq), 0)
            col = lax.broadcasted_iota(jnp.int32, (tq, tq), 1)
            s = jnp.where(col <= row, s, -jnp.inf)
        m_new = jnp.maximum(m, jnp.max(s, axis=-1, keepdims=True))
        alpha = jnp.exp(m - m_new)
        p = jnp.exp(s - m_new)
        l = alpha * l + jnp.sum(p, axis=-1, keepdims=True)
        acc = alpha * acc + _dot(p.astype(BF16), v_ref[pl.ds(k0, tq), :])
        return m_new, l, acc

    init = (jnp.full((tq, 1), -jnp.inf, F32), jnp.zeros((tq, 1), F32), jnp.zeros((tq, hd), F32))
    carry = lax.fori_loop(0, qi, lambda kj, c: step(kj, c, False), init)
    _, l, acc = step(qi, carry, True)
    o_ref[...] = (acc / l).astype(o_ref.dtype)


def _flash_attention(q, k, v, c, heads):
    nb, nt, d = q.shape
    hd = d // heads
    tq = _tile(nt, 512)
    return pl.pallas_call(
        functools.partial(_flash_kernel, tq=tq, scale=hd ** -0.5),
        grid=(nb, heads, nt // tq),
        in_specs=[pl.BlockSpec((None, tq, hd), lambda b, h, i: (b, i, h)),
                  pl.BlockSpec((None, nt, hd), lambda b, h, i: (b, 0, h)),
                  pl.BlockSpec((None, nt, hd), lambda b, h, i: (b, 0, h)),
                  pl.BlockSpec((None, None, 1, nt), lambda b, h, i: (b, h, 0, 0))],
        out_specs=pl.BlockSpec((None, tq, hd), lambda b, h, i: (b, i, h)),
        out_shape=jax.ShapeDtypeStruct((nb, nt, d), BF16),
        compiler_params=_params("parallel", "parallel", "arbitrary"),
        name="fox_flash",
    )(q, k, v, c)


def _decode_kernel(q_ref, kc_ref, vc_ref, kn_ref, vn_ref, cp_ref, cn_ref, o_ref, *, scale):
    q = q_ref[...]
    ts = q.shape[0]
    s_p = _dot_nt(q, kc_ref[...].astype(BF16)) * scale - cp_ref[...]
    s_n = _dot_nt(q, kn_ref[...]) * scale - cn_ref[...]
    row = lax.broadcasted_iota(jnp.int32, (ts, ts), 0)
    col = lax.broadcasted_iota(jnp.int32, (ts, ts), 1)
    s_n = jnp.where(col <= row, s_n, -jnp.inf)
    m = jnp.maximum(jnp.max(s_p, axis=-1, keepdims=True), jnp.max(s_n, axis=-1, keepdims=True))
    p_p = jnp.exp(s_p - m)
    p_n = jnp.exp(s_n - m)
    l = jnp.sum(p_p, axis=-1, keepdims=True) + jnp.sum(p_n, axis=-1, keepdims=True)
    acc = _dot(p_p.astype(BF16), vc_ref[...].astype(BF16)) + _dot(p_n.astype(BF16), vn_ref[...])
    o_ref[...] = (acc / l).astype(o_ref.dtype)


def _decode_attention(q, k_cache, v_cache, k_new, v_new, c_past, c_new, heads):
    nb, ts, d = q.shape
    p = k_cache.shape[1]
    hd = d // heads
    return pl.pallas_call(
        functools.partial(_decode_kernel, scale=hd ** -0.5),
        grid=(nb, heads),
        in_specs=[pl.BlockSpec((None, ts, hd), lambda b, h: (b, 0, h)),
                  pl.BlockSpec((None, p, hd), lambda b, h: (b, 0, h)),
                  pl.BlockSpec((None, p, hd), lambda b, h: (b, 0, h)),
                  pl.BlockSpec((None, ts, hd), lambda b, h: (b, 0, h)),
                  pl.BlockSpec((None, ts, hd), lambda b, h: (b, 0, h)),
                  pl.BlockSpec((None, None, 1, p), lambda b, h: (b, h, 0, 0)),
                  pl.BlockSpec((None, None, 1, ts), lambda b, h: (b, h, 0, 0))],
        out_specs=pl.BlockSpec((None, ts, hd), lambda b, h: (b, 0, h)),
        out_shape=jax.ShapeDtypeStruct((nb, ts, d), BF16),
        compiler_params=_params("parallel", "parallel"),
        name="fox_decode",
    )(q, k_cache, v_cache, k_new, v_new, c_past, c_new)


def _gla_kernel(q_ref, k_ref, v_ref, la_ref, s0_ref, o_ref, s_out_ref, s_scr, *, chunk, sub, nsteps, scale):
    c = pl.program_id(2)

    @pl.when(c == 0)
    def _():
        s_scr[...] = s0_ref[...]

    q = q_ref[...] * scale
    k = k_ref[...]
    v = v_ref[...].astype(BF16)
    row = lax.broadcasted_iota(jnp.int32, q.shape, 0)
    b = la_ref[...]
    s = 1
    while s < chunk:
        b = b + jnp.where(row >= s, pltpu.roll(b, s, axis=0), 0.0)
        s *= 2
    b_last = b[chunk - 1:chunk, :]
    st = s_scr[...]

    o = _dot_nt((q * jnp.exp(b)).astype(BF16), st.astype(BF16))

    lane = lax.broadcasted_iota(jnp.int32, (sub, chunk), 1)
    srow = lax.broadcasted_iota(jnp.int32, (sub, chunk), 0)
    blocks = []
    for i in range(chunk // sub):
        r0 = i * sub
        qi = q[r0:r0 + sub]
        bi = b[r0:r0 + sub]
        diag = jnp.zeros((sub, chunk), F32)
        for j in range(sub):
            kj = k[r0 + j:r0 + j + 1]
            bj = b[r0 + j:r0 + j + 1]
            e = jnp.exp(jnp.minimum(bi - bj, 0.0))
            col = jnp.sum(qi * kj * e, axis=-1, keepdims=True)
            diag = jnp.where(lane == r0 + j, col, diag)
        a_i = jnp.where(lane - r0 <= srow, diag, 0.0)
        if i > 0:
            ref_b = b[r0:r0 + 1]
            qg = (qi * jnp.exp(bi - ref_b)).astype(BF16)
            kg = (k * jnp.exp(jnp.minimum(ref_b - b, 0.0))).astype(BF16)
            a_i = jnp.where(lane < r0, _dot_nt(qg, kg), a_i)
        blocks.append(a_i)
    a = jnp.concatenate(blocks, axis=0)
    o_ref[...] = o + _dot(a.astype(BF16), v)

    kd = (k * jnp.exp(b_last - b)).astype(BF16)
    st_new = st * jnp.exp(b_last) + _dot_tn(v, kd)
    s_scr[...] = st_new

    @pl.when(c == nsteps - 1)
    def _():
        s_out_ref[...] = st_new


def _gla(qkvr, la, s0t, heads, dk, dv, chunk):
    nb, nt, _ = qkvr.shape
    nsteps = nt // chunk
    kb = heads
    vb = 2 * heads * dk // dv
    assert (2 * heads * dk) % dv == 0
    return pl.pallas_call(
        functools.partial(_gla_kernel, chunk=chunk, sub=min(GLA_SUB, chunk), nsteps=nsteps, scale=dk ** -0.5),
        grid=(nb, heads, nsteps),
        in_specs=[pl.BlockSpec((None, chunk, dk), lambda b, h, c: (b, c, h)),
                  pl.BlockSpec((None, chunk, dk), lambda b, h, c: (b, c, kb + h)),
                  pl.BlockSpec((None, chunk, dv), lambda b, h, c: (b, c, vb + h)),
                  pl.BlockSpec((None, chunk, dk), lambda b, h, c: (b, c, h)),
                  pl.BlockSpec((None, None, dv, dk), lambda b, h, c: (b, h, 0, 0))],
        out_specs=[pl.BlockSpec((None, chunk, dv), lambda b, h, c: (b, c, h)),
                   pl.BlockSpec((None, None, dv, dk), lambda b, h, c: (b, h, 0, 0))],
        out_shape=[jax.ShapeDtypeStruct((nb, nt, heads * dv), F32),
                   jax.ShapeDtypeStruct((nb, heads, dv, dk), F32)],
        scratch_shapes=[pltpu.VMEM((dv, dk), F32)],
        compiler_params=_params("parallel", "parallel", "arbitrary"),
        name="gla_chunk",
    )(qkvr, qkvr, qkvr, la, s0t)


def _gla_out_kernel(o_ref, r_ref, g_ref, a_ref, *, heads, dv):
    g = g_ref[...]
    for h in range(heads):
        sl = slice(h * dv, (h + 1) * dv)
        x = o_ref[:, sl]
        r = r_ref[:, sl]
        ms = jnp.mean(x * x, axis=-1, keepdims=True)
        a_ref[:, sl] = (x * lax.rsqrt(ms + EPS) * g * (r * jax.nn.sigmoid(r))).astype(a_ref.dtype)


def _gla_out(o, qkvr, g_o, heads, dv):
    m, d = o.shape
    tm = _tile(m, 512)
    rb = qkvr.shape[1] // d - 1
    return pl.pallas_call(
        functools.partial(_gla_out_kernel, heads=heads, dv=dv),
        grid=(m // tm,),
        in_specs=[pl.BlockSpec((tm, d), lambda i: (i, 0)),
                  pl.BlockSpec((tm, d), lambda i: (i, rb)),
                  pl.BlockSpec((1, dv), lambda i: (0, 0))],
        out_specs=pl.BlockSpec((tm, d), lambda i: (i, 0)),
        out_shape=jax.ShapeDtypeStruct((m, d), BF16),
        compiler_params=_params("parallel"),
        name="gla_out_norm",
    )(o, qkvr, g_o.reshape(1, dv))


def _conv_layer(x, nb, nt, hist, g_mix, w_in, w_dw, g_norm, w_out):
    m, d = x.shape
    width = w_dw.shape[0]
    h = _rmsnorm(x, g_mix)
    (u,) = _mm(_mm_glu_kernel, h, [(w_in, 0), (w_in, d)], [], [F32], d)
    u3 = u.reshape(nb, nt, d)
    if hist is None:
        hist_pad = jnp.zeros((nb, CONV_HALO, d), F32)
        tail = u3[:, nt - (width - 1):]
    else:
        hist_pad = jnp.pad(hist, ((0, 0), (CONV_HALO - (width - 1), 0), (0, 0)))
        tail = jnp.concatenate([hist, u3], axis=1)[:, nt:]
    a = _conv_norm_silu(u3, hist_pad, w_dw, g_norm)
    (y,) = _mm(_mm_res_kernel, a.reshape(m, d), [(w_out, 0)], [(x, "tile")], [F32], d)
    return y, tail


def _pad_cols(w, n):
    return jnp.pad(w, ((0, 0), (0, n - w.shape[1])))


def _fox_project(x, g_mix, w_qkv, w_f, b_f, g_q, g_k, heads):
    m, d = x.shape
    hd = d // heads
    h = _rmsnorm(x, g_mix)
    tn = 512 if d % 512 == 0 else d
    hn = functools.partial(_mm_headnorm_kernel, heads_per_tile=tn // hd, head_dim=hd)
    (q,) = _mm(hn, h, [(w_qkv, 0)], [(g_q.reshape(1, hd), "const")], [BF16], d, tn_pref=tn)
    k32, k16 = _mm(hn, h, [(w_qkv, d)], [(g_k.reshape(1, hd), "const")], [F32, BF16], d, tn_pref=tn)
    v32, v16 = _mm(_mm_plain_kernel, h, [(w_qkv, 2 * d)], [], [F32, BF16], d, tn_pref=tn)
    w_fp = _pad_cols(w_f, LANES)
    b_fp = jnp.pad(b_f, (0, LANES - heads)).reshape(1, LANES)
    (lf,) = _mm(_mm_logsig_kernel, h, [(w_fp, 0)], [(b_fp, "row")], [F32], LANES)
    return q, k32, k16, v32, v16, lf[:, :heads]


def _fox_layer(xp, xs, dims, cache_k, cache_v, cache_logf, g_mix, w_qkv, w_f, b_f, g_q, g_k, w_o):
    nb, nt, nbs, nts = dims
    d = xp.shape[1]
    heads, hd = cache_k.shape[2], cache_k.shape[3]
    past = cache_k.shape[1]
    w_qkv = w_qkv.astype(BF16)
    w_f = w_f.astype(BF16)
    w_o = w_o.astype(BF16)

    q, k32, k16, v32, v16, lf = _fox_project(xp, g_mix, w_qkv, w_f, b_f, g_q, g_k, heads)
    c = _cumsum_lanes(jnp.transpose(lf.reshape(nb, nt, heads), (0, 2, 1)))
    o = _flash_attention(q.reshape(nb, nt, d), k16.reshape(nb, nt, d), v16.reshape(nb, nt, d),
                         c.reshape(nb, heads, 1, nt), heads)
    (yp,) = _mm(_mm_res_kernel, o.reshape(nb * nt, d), [(w_o, 0)], [(xp, "tile")], [F32], d)
    new_p = (k32.reshape(nb, nt, heads, hd), v32.reshape(nb, nt, heads, hd), lf.reshape(nb, nt, heads))

    qs, ks32, ks16, vs32, vs16, lfs = _fox_project(xs, g_mix, w_qkv, w_f, b_f, g_q, g_k, heads)
    c_past = _cumsum_lanes(jnp.transpose(cache_logf, (0, 2, 1)), sub_last=True)
    lfs_t = jnp.transpose(lfs.reshape(nbs, nts, heads), (0, 2, 1))
    nts_pad = -(-nts // LANES) * LANES
    c_new = _cumsum_lanes(jnp.pad(lfs_t, ((0, 0), (0, 0), (0, nts_pad - nts))))[:, :, :nts]
    os_ = _decode_attention(qs.reshape(nbs, nts, d), cache_k.reshape(nbs, past, d), cache_v.reshape(nbs, past, d),
                            ks16.reshape(nbs, nts, d), vs16.reshape(nbs, nts, d),
                            c_past.reshape(nbs, heads, 1, past), c_new.reshape(nbs, heads, 1, nts), heads)
    (ys,) = _mm(_mm_res_kernel, os_.reshape(nbs * nts, d), [(w_o, 0)], [(xs, "tile")], [F32], d)
    new_s = (ks32.reshape(nbs, nts, heads, hd), vs32.reshape(nbs, nts, heads, hd), lfs.reshape(nbs, nts, heads))
    return yp, ys, new_p + new_s


def _gla_group(x, nb, nt, s0t, chunk, heads, dk, dv, g_mix, w_qkvr, w_a1p, w_a2p, b_a, g_o, w_o):
    m, d = x.shape
    h = _rmsnorm(x, g_mix)
    n = w_qkvr.shape[1]
    (qkvr,) = _mm(_mm_plain_kernel, h, [(w_qkvr, 0)], [], [F32], n)
    la = _gate(h, w_a1p, w_a2p, b_a)
    o, st = _gla(qkvr.reshape(nb, nt, n), la.reshape(nb, nt, heads * dk), s0t, heads, dk, dv, chunk)
    a = _gla_out(o.reshape(m, d), qkvr, g_o, heads, dv)
    (y,) = _mm(_mm_res_kernel, a, [(w_o, 0)], [(x, "tile")], [F32], d)
    return y, jnp.swapaxes(st, 2, 3)


def _gla_layer(xp, xs, dims, state, g_mix, w_qkvr, w_a1, w_a2, b_a, g_o, w_o):
    nb, nt, nbs, nts = dims
    heads, dk, dv = state.shape[1], state.shape[2], state.shape[3]
    rank = w_a1.shape[1]
    w_qkvr = w_qkvr.astype(BF16)
    w_o = w_o.astype(BF16)
    w_a1p = _pad_cols(w_a1, LANES).astype(BF16)
    w_a2p = jnp.pad(w_a2, ((0, LANES - rank), (0, 0))).astype(BF16)
    common = (heads, dk, dv, g_mix, w_qkvr, w_a1p, w_a2p, b_a, g_o, w_o)
    zeros = jnp.zeros((nb, heads, dv, dk), F32)
    yp, sp = _gla_group(xp, nb, nt, zeros, GLA_CHUNK, *common)
    ys, ss = _gla_group(xs, nbs, nts, jnp.swapaxes(state, 2, 3), nts, *common)
    return yp, ys, (sp, ss)


def kernel(x_prompt, x_sample, cache_conv_l0, cache_k_l1, cache_v_l1, cache_logf_l1, state_gla_l2, cache_conv_l3, norm_mix_l0, conv_w_in_l0, conv_w_dw_l0, conv_norm_l0, conv_w_out_l0, norm_ffn_l0, ffn_w_up_l0, ffn_w_down_l0, norm_mix_l1, fox_w_qkv_l1, fox_w_f_l1, fox_b_f_l1, fox_q_norm_l1, fox_k_norm_l1, fox_w_o_l1, norm_ffn_l1, ffn_w_up_l1, ffn_w_down_l1, norm_mix_l2, gla_w_qkvr_l2, gla_w_a1_l2, gla_w_a2_l2, gla_b_a_l2, gla_o_norm_l2, gla_w_o_l2, norm_ffn_l2, ffn_w_up_l2, ffn_w_down_l2, norm_mix_l3, conv_w_in_l3, conv_w_dw_l3, conv_norm_l3, conv_w_out_l3, norm_ffn_l3, ffn_w_up_l3, ffn_w_down_l3):
    nb, nt, d = x_prompt.shape
    nbs, nts, _ = x_sample.shape
    dims = (nb, nt, nbs, nts)
    xp = x_prompt.reshape(nb * nt, d)
    xs = x_sample.reshape(nbs * nts, d)

    def ffn(xp, xs, g, wu, wd):
        wu = wu.astype(BF16)
        wd = wd.astype(BF16)
        return _ffn(xp, g, wu, wd), _ffn(xs, g, wu, wd)

    def conv(xp, xs, cache, g_mix, w_in, w_dw, g_norm, w_out):
        w_in = w_in.astype(BF16)
        w_out = w_out.astype(BF16)
        yp, sp = _conv_layer(xp, nb, nt, None, g_mix, w_in, w_dw, g_norm, w_out)
        ys, ss = _conv_layer(xs, nbs, nts, cache, g_mix, w_in, w_dw, g_norm, w_out)
        return yp, ys, (sp, ss)

    xp, xs, st0 = conv(xp, xs, cache_conv_l0, norm_mix_l0, conv_w_in_l0, conv_w_dw_l0, conv_norm_l0, conv_w_out_l0)
    xp, xs = ffn(xp, xs, norm_ffn_l0, ffn_w_up_l0, ffn_w_down_l0)
    xp, xs, st1 = _fox_layer(xp, xs, dims, cache_k_l1, cache_v_l1, cache_logf_l1, norm_mix_l1, fox_w_qkv_l1,
                             fox_w_f_l1, fox_b_f_l1, fox_q_norm_l1, fox_k_norm_l1, fox_w_o_l1)
    xp, xs = ffn(xp, xs, norm_ffn_l1, ffn_w_up_l1, ffn_w_down_l1)
    xp, xs, st2 = _gla_layer(xp, xs, dims, state_gla_l2, norm_mix_l2, gla_w_qkvr_l2, gla_w_a1_l2, gla_w_a2_l2,
                             gla_b_a_l2, gla_o_norm_l2, gla_w_o_l2)
    xp, xs = ffn(xp, xs, norm_ffn_l2, ffn_w_up_l2, ffn_w_down_l2)
    xp, xs, st3 = conv(xp, xs, cache_conv_l3, norm_mix_l3, conv_w_in_l3, conv_w_dw_l3, conv_norm_l3, conv_w_out_l3)
    xp, xs = ffn(xp, xs, norm_ffn_l3, ffn_w_up_l3, ffn_w_down_l3)
    return (xp.reshape(nb, nt, d), xs.reshape(nbs, nts, d), *st0, *st1, *st2, *st3)
```

```python
import functools
import math

import jax
import jax.numpy as jnp
from jax import lax
from jax.experimental import pallas as pl
from jax.experimental.pallas import tpu as pltpu

F32 = jnp.float32
BF16 = jnp.bfloat16

EPS = 1e-6
LOG2E = math.log2(math.e)
GLA_CHUNK = 64
GLA_SUB = 16
GLA_CHUNKS_PER_STEP = 2
GLA_TAU = 16.0
FLASH_ROW_CHUNK = 32
LANES = 128
SUBLANES = 8
CONV_HALO = 32
VMEM_LIMIT_BYTES = 56 * 1024 * 1024


def _tile(n, pref):
    if n <= pref:
        return n
    t = pref - pref % 8
    while t >= 8:
        if n % t == 0:
            return t
        t -= 8
    return n


def _params(*sem):
    return pltpu.CompilerParams(dimension_semantics=sem, vmem_limit_bytes=VMEM_LIMIT_BYTES)


def _log_sigmoid(x):
    return jnp.minimum(x, 0.0) - jnp.log1p(jnp.exp(-jnp.abs(x)))


def _rms_bf16(x, g):
    ms = jnp.mean(x * x, axis=-1, keepdims=True)
    return (x * lax.rsqrt(ms + EPS) * g).astype(BF16)


def _dot(a, b):
    return jnp.dot(a, b, preferred_element_type=F32)


def _dot_nt(a, b):
    return lax.dot_general(a, b, (((1,), (1,)), ((), ())), preferred_element_type=F32)


def _dot_tn(a, b):
    return lax.dot_general(a, b, (((0,), (0,)), ((), ())), preferred_element_type=F32)


def _conv_in_kernel(x_ref, g_ref, wa_ref, wb_ref, o_ref, h_scr):
    @pl.when(pl.program_id(1) == 0)
    def _():
        h_scr[...] = _rms_bf16(x_ref[...], g_ref[...])

    h = h_scr[...]
    o_ref[...] = _dot(h, wa_ref[...]) * jax.nn.sigmoid(_dot(h, wb_ref[...]))


def _conv_in(x, g, w_in):
    m, d = x.shape
    n = w_in.shape[1] // 2
    tm = _tile(m, 1024)
    tn = _tile(n, 512)
    nb = n // tn
    return pl.pallas_call(
        _conv_in_kernel,
        grid=(m // tm, nb),
        in_specs=[pl.BlockSpec((tm, d), lambda i, j: (i, 0)),
                  pl.BlockSpec((1, d), lambda i, j: (0, 0)),
                  pl.BlockSpec((d, tn), lambda i, j: (0, j)),
                  pl.BlockSpec((d, tn), lambda i, j: (0, j + nb))],
        out_specs=pl.BlockSpec((tm, tn), lambda i, j: (i, j)),
        out_shape=jax.ShapeDtypeStruct((m, n), F32),
        scratch_shapes=[pltpu.VMEM((tm, d), BF16)],
        compiler_params=_params("parallel", "arbitrary"),
        name="conv_in_glu",
    )(x, g.reshape(1, d), w_in, w_in)


def _fox_proj_kernel(x_ref, g_ref, w_ref, gain_ref, wf_ref, bf_ref, qkv_ref, lf_ref, h_scr, *,
                     norm_tiles, heads_per_tile, head_dim):
    j = pl.program_id(1)

    @pl.when(j == 0)
    def _():
        h = _rms_bf16(x_ref[...], g_ref[...])
        h_scr[...] = h
        lf_ref[...] = _log_sigmoid(_dot(h, wf_ref[...]) + bf_ref[...])

    acc = _dot(h_scr[...], w_ref[...])

    @pl.when(j < norm_tiles)
    def _():
        gain = gain_ref[...]
        for h in range(heads_per_tile):
            sl = slice(h * head_dim, (h + 1) * head_dim)
            x = acc[:, sl]
            ms = jnp.mean(x * x, axis=-1, keepdims=True)
            qkv_ref[:, sl] = x * lax.rsqrt(ms + EPS) * gain[:, sl]

    @pl.when(j >= norm_tiles)
    def _():
        qkv_ref[...] = acc


def _fox_proj(x, g, w_qkv, gain_row, w_fp, b_fp, head_dim):
    m, d = x.shape
    n = w_qkv.shape[1]
    tm = _tile(m, 1024)
    tn = _tile(d, 1024)
    assert tn % head_dim == 0
    return pl.pallas_call(
        functools.partial(_fox_proj_kernel, norm_tiles=2 * d // tn, heads_per_tile=tn // head_dim, head_dim=head_dim),
        grid=(m // tm, n // tn),
        in_specs=[pl.BlockSpec((tm, d), lambda i, j: (i, 0)),
                  pl.BlockSpec((1, d), lambda i, j: (0, 0)),
                  pl.BlockSpec((d, tn), lambda i, j: (0, j)),
                  pl.BlockSpec((1, tn), lambda i, j: (0, j)),
                  pl.BlockSpec((d, LANES), lambda i, j: (0, 0)),
                  pl.BlockSpec((1, LANES), lambda i, j: (0, 0))],
        out_specs=[pl.BlockSpec((tm, tn), lambda i, j: (i, j)),
                   pl.BlockSpec((tm, LANES), lambda i, j: (i, 0))],
        out_shape=[jax.ShapeDtypeStruct((m, n), F32), jax.ShapeDtypeStruct((m, LANES), F32)],
        scratch_shapes=[pltpu.VMEM((tm, d), BF16)],
        compiler_params=_params("parallel", "arbitrary"),
        name="fox_proj",
    )(x, g.reshape(1, d), w_qkv, gain_row, w_fp, b_fp)


def _gla_proj_kernel(x_ref, g_ref, w_ref, w1_ref, w2_ref, ba_ref, o_ref, la_ref, h_scr):
    @pl.when(pl.program_id(1) == 0)
    def _():
        h = _rms_bf16(x_ref[...], g_ref[...])
        h_scr[...] = h
        low = _dot(h, w1_ref[...]).astype(BF16)
        la_ref[...] = _log_sigmoid(_dot(low, w2_ref[...]) + ba_ref[...]) * (1.0 / GLA_TAU)

    o_ref[...] = _dot(h_scr[...], w_ref[...])


def _gla_proj(x, g, w_qkvr, w_a1p, w_a2p, b_a):
    m, d = x.shape
    n = w_qkvr.shape[1]
    na = w_a2p.shape[1]
    tm = _tile(m, 1024)
    tn = _tile(n, 1024)
    return pl.pallas_call(
        _gla_proj_kernel,
        grid=(m // tm, n // tn),
        in_specs=[pl.BlockSpec((tm, d), lambda i, j: (i, 0)),
                  pl.BlockSpec((1, d), lambda i, j: (0, 0)),
                  pl.BlockSpec((d, tn), lambda i, j: (0, j)),
                  pl.BlockSpec(w_a1p.shape, lambda i, j: (0, 0)),
                  pl.BlockSpec(w_a2p.shape, lambda i, j: (0, 0)),
                  pl.BlockSpec((1, na), lambda i, j: (0, 0))],
        out_specs=[pl.BlockSpec((tm, tn), lambda i, j: (i, j)),
                   pl.BlockSpec((tm, na), lambda i, j: (i, 0))],
        out_shape=[jax.ShapeDtypeStruct((m, n), F32), jax.ShapeDtypeStruct((m, na), F32)],
        scratch_shapes=[pltpu.VMEM((tm, d), BF16)],
        compiler_params=_params("parallel", "arbitrary"),
        name="gla_proj",
    )(x, g.reshape(1, d), w_qkvr, w_a1p, w_a2p, b_a.reshape(1, na))


def _out_proj_kernel(a_ref, w_ref, r_ref, o_ref):
    o_ref[...] = r_ref[...] + _dot(a_ref[...], w_ref[...])


def _out_proj(a, w, res):
    m, k = a.shape
    n = w.shape[1]
    tm = _tile(m, 1024)
    tn = _tile(n, 1024)
    return pl.pallas_call(
        _out_proj_kernel,
        grid=(m // tm, n // tn),
        in_specs=[pl.BlockSpec((tm, k), lambda i, j: (i, 0)),
                  pl.BlockSpec((k, tn), lambda i, j: (0, j)),
                  pl.BlockSpec((tm, tn), lambda i, j: (i, j))],
        out_specs=pl.BlockSpec((tm, tn), lambda i, j: (i, j)),
        out_shape=jax.ShapeDtypeStruct((m, n), F32),
        compiler_params=_params("parallel", "arbitrary"),
        name="out_proj",
    )(a, w, res)


def _ffn_kernel(x_ref, g_ref, wu_ref, wd_ref, o_ref, h_scr):
    @pl.when(pl.program_id(1) == 0)
    def _():
        x = x_ref[...]
        h_scr[...] = _rms_bf16(x, g_ref[...])
        o_ref[...] = x

    u = jnp.maximum(_dot(h_scr[...], wu_ref[...]), 0.0)
    u = (u * u).astype(BF16)
    d = o_ref.shape[1]
    dc = min(d, 512)
    for c in range(d // dc):
        cs = slice(c * dc, (c + 1) * dc)
        o_ref[:, cs] += _dot(u, wd_ref[:, cs])


def _ffn(x, g, wu, wd):
    m, d = x.shape
    f = wu.shape[1]
    tm = _tile(m, 1024)
    tf = _tile(f, 512)
    return pl.pallas_call(
        _ffn_kernel,
        grid=(m // tm, f // tf),
        in_specs=[pl.BlockSpec((tm, d), lambda i, j: (i, 0)),
                  pl.BlockSpec((1, d), lambda i, j: (0, 0)),
                  pl.BlockSpec((d, tf), lambda i, j: (0, j)),
                  pl.BlockSpec((tf, d), lambda i, j: (j, 0))],
        out_specs=pl.BlockSpec((tm, d), lambda i, j: (i, 0)),
        out_shape=jax.ShapeDtypeStruct((m, d), F32),
        scratch_shapes=[pltpu.VMEM((tm, d), BF16)],
        compiler_params=_params("parallel", "arbitrary"),
        name="ffn",
    )(x, g.reshape(1, d), wu, wd)


def _shift_rows_up(x, r):
    n, c = x.shape
    x3 = x.reshape(n // SUBLANES, SUBLANES, c)
    rot = pltpu.roll(x3, SUBLANES - r, axis=1)
    sub = lax.broadcasted_iota(jnp.int32, (n // SUBLANES - 1, SUBLANES, c), 1)
    return jnp.where(sub < SUBLANES - r, rot[:-1], rot[1:]).reshape(n - SUBLANES, c)


def _conv_kernel(u_ref, up_ref, hist_ref, w_ref, g_ref, o_ref, ext_scr, y_scr, *, tm, width, rb, lc):
    i = pl.program_id(1)
    d = u_ref.shape[-1]
    nr = 2 * SUBLANES

    @pl.when(i == 0)
    def _():
        ext_scr[0:CONV_HALO, :] = hist_ref[...]

    @pl.when(i > 0)
    def _():
        ext_scr[0:CONV_HALO, :] = up_ref[...]

    ext_scr[CONV_HALO:CONV_HALO + tm, :] = u_ref[...]
    off = CONV_HALO - (width - 1)

    def row_block(r, carry):
        r0 = pl.multiple_of(r * rb, rb)
        for c in range(d // lc):
            cs = slice(c * lc, (c + 1) * lc)
            acc = jnp.zeros((rb, lc), F32)
            for rem in range(SUBLANES):
                part = None
                for k in range(width):
                    if (off + k) % SUBLANES != rem:
                        continue
                    base = (off + k) // SUBLANES * SUBLANES
                    rows = rb + SUBLANES if rem else rb
                    term = ext_scr[pl.ds(r0 + base, rows), cs] * w_ref[k:k + 1, cs]
                    part = term if part is None else part + term
                if part is None:
                    continue
                acc = acc + (_shift_rows_up(part, rem) if rem else part)
            y_scr[pl.ds(r0, rb), cs] = acc
        for s in range(rb // nr):
            rows = pl.ds(r0 + s * nr, nr)
            y = y_scr[rows, :]
            ms = jnp.mean(y * y, axis=-1, keepdims=True)
            z = y * lax.rsqrt(ms + EPS) * g_ref[...]
            o_ref[rows, :] = (z * jax.nn.sigmoid(z)).astype(o_ref.dtype)
        return carry

    lax.fori_loop(0, tm // rb, row_block, 0)


def _conv_norm_silu(u, hist_pad, w_dw, g):
    nb, nt, d = u.shape
    width = w_dw.shape[0]
    assert width - 1 <= CONV_HALO
    tm = _tile(nt, 512)
    assert tm % CONV_HALO == 0
    rb = 64 if tm % 64 == 0 else CONV_HALO
    lc = min(256, d)
    hb = tm // CONV_HALO
    return pl.pallas_call(
        functools.partial(_conv_kernel, tm=tm, width=width, rb=rb, lc=lc),
        grid=(nb, nt // tm),
        in_specs=[pl.BlockSpec((None, tm, d), lambda b, i: (b, i, 0)),
                  pl.BlockSpec((None, CONV_HALO, d), lambda b, i: (b, jnp.maximum(i * hb - 1, 0), 0)),
                  pl.BlockSpec((None, CONV_HALO, d), lambda b, i: (b, 0, 0)),
                  pl.BlockSpec((width, d), lambda b, i: (0, 0)),
                  pl.BlockSpec((1, d), lambda b, i: (0, 0))],
        out_specs=pl.BlockSpec((None, tm, d), lambda b, i: (b, i, 0)),
        out_shape=jax.ShapeDtypeStruct((nb, nt, d), BF16),
        scratch_shapes=[pltpu.VMEM((CONV_HALO + tm, d), F32), pltpu.VMEM((tm, d), F32)],
        compiler_params=_params("parallel", "arbitrary"),
        name="conv_norm_silu",
    )(u, u, hist_pad, w_dw, g.reshape(1, d))


def _cumsum_kernel(x_ref, o_ref, *, n, sub_last, scale):
    x = x_ref[...]
    lane = lax.broadcasted_iota(jnp.int32, x.shape, 1)
    s = 1
    while s < n:
        x = x + jnp.where(lane >= s, pltpu.roll(x, s, axis=1), 0.0)
        s *= 2
    if sub_last:
        x = x - x[:, n - 1:n]
    o_ref[...] = x * scale


def _cumsum_lanes(x, *, sub_last=False, scale=LOG2E):
    nb, h, n = x.shape
    return pl.pallas_call(
        functools.partial(_cumsum_kernel, n=n, sub_last=sub_last, scale=scale),
        grid=(nb,),
        in_specs=[pl.BlockSpec((None, h, n), lambda b: (b, 0, 0))],
        out_specs=pl.BlockSpec((None, h, n), lambda b: (b, 0, 0)),
        out_shape=jax.ShapeDtypeStruct((nb, h, n), F32),
        compiler_params=_params("parallel"),
        name="cumsum_logf",
    )(x)


def _flash_kernel(q_ref, k_ref, v_ref, c_ref, o_ref, k_scr, v_scr, s_scr, p_scr, m_scr, l_scr, acc_scr, *,
                  tq, hd, hb, rc):
    qi = pl.program_id(2)

    @pl.when(qi == 0)
    def _():
        k_scr[...] = k_ref[...].astype(BF16)
        v_scr[...] = v_ref[...].astype(BF16)

    heads = [slice(h * hd, (h + 1) * hd) for h in range(hb)]
    qs = [q_ref[:, hs].astype(BF16) for hs in heads]
    m_scr[...] = jnp.full(m_scr.shape, -jnp.inf, F32)
    l_scr[...] = jnp.zeros(l_scr.shape, F32)
    acc_scr[...] = jnp.zeros(acc_scr.shape, F32)

    def step(kj, masked):
        k0 = pl.multiple_of(kj * tq, tq)
        for h, hs in enumerate(heads):
            s_scr[h] = _dot_nt(qs[h], k_scr[pl.ds(k0, tq), hs])
            c_row = c_ref[h:h + 1, pl.ds(k0, tq)]
            m_all = m_scr[h]
            l_all = l_scr[h]
            m_out, l_out, a_out = [], [], []
            for r in range(tq // rc):
                rows = slice(r * rc, (r + 1) * rc)
                blocks = []
                for i in range(tq // LANES):
                    cols = slice(i * LANES, (i + 1) * LANES)
                    blk = s_scr[h, rows, cols] - c_row[:, cols]
                    if masked:
                        row = lax.broadcasted_iota(jnp.int32, (rc, LANES), 0) + r * rc
                        col = lax.broadcasted_iota(jnp.int32, (rc, LANES), 1) + i * LANES
                        blk = jnp.where(col <= row, blk, -jnp.inf)
                    blocks.append(blk)
                m_old = m_all[rows]
                m_new = jnp.maximum(m_old, jnp.max(functools.reduce(jnp.maximum, blocks), axis=-1, keepdims=True))
                alpha = jnp.exp2(m_old - m_new)
                ps = [jnp.exp2(blk - m_new) for blk in blocks]
                l_out.append(alpha * l_all[rows] + jnp.sum(functools.reduce(jnp.add, ps), axis=-1, keepdims=True))
                m_out.append(m_new)
                a_out.append(alpha)
                for i, pb in enumerate(ps):
                    p_scr[h, rows, i * LANES:(i + 1) * LANES] = pb.astype(BF16)
            m_scr[h] = jnp.concatenate(m_out, axis=0)
            l_scr[h] = jnp.concatenate(l_out, axis=0)
            alpha = jnp.concatenate(a_out, axis=0)
            acc_scr[h] = alpha * acc_scr[h] + _dot(p_scr[h], v_scr[pl.ds(k0, tq), hs])

    def body(kj, carry):
        step(kj, False)
        return carry

    lax.fori_loop(0, qi, body, 0)
    step(qi, True)
    for h, hs in enumerate(heads):
        o_ref[:, hs] = (acc_scr[h] / l_scr[h]).astype(o_ref.dtype)


def _flash_attention(qkv, c, heads, hb):
    nb, nt, d3 = qkv.shape
    d = d3 // 3
    hd = d // heads
    assert hd == LANES
    tq = _tile(nt, 512)
    w = hb * hd
    nhb = heads // hb
    return pl.pallas_call(
        functools.partial(_flash_kernel, tq=tq, hd=hd, hb=hb, rc=min(FLASH_ROW_CHUNK, tq)),
        grid=(nb, nhb, nt // tq),
        in_specs=[pl.BlockSpec((None, tq, w), lambda b, h, i: (b, i, h)),
                  pl.BlockSpec((None, nt, w), lambda b, h, i: (b, 0, nhb + h)),
                  pl.BlockSpec((None, nt, w), lambda b, h, i: (b, 0, 2 * nhb + h)),
                  pl.BlockSpec((None, None, hb, nt), lambda b, h, i: (b, h, 0, 0))],
        out_specs=pl.BlockSpec((None, tq, w), lambda b, h, i: (b, i, h)),
        out_shape=jax.ShapeDtypeStruct((nb, nt, d), BF16),
        scratch_shapes=[pltpu.VMEM((nt, w), BF16), pltpu.VMEM((nt, w), BF16),
                        pltpu.VMEM((hb, tq, tq), F32), pltpu.VMEM((hb, tq, tq), BF16),
                        pltpu.VMEM((hb, tq, LANES), F32), pltpu.VMEM((hb, tq, LANES), F32),
                        pltpu.VMEM((hb, tq, hd), F32)],
        compiler_params=_params("parallel", "parallel", "arbitrary"),
        name="fox_flash",
    )(qkv, qkv, qkv, c)


def _decode_kernel(q_ref, kn_ref, vn_ref, kc_ref, vc_ref, cp_ref, cn_ref, o_ref):
    q = q_ref[...].astype(BF16)
    ts = q.shape[0]
    s_p = _dot_nt(q, kc_ref[...].astype(BF16)) - cp_ref[...]
    s_n = _dot_nt(q, kn_ref[...].astype(BF16)) - cn_ref[...]
    row = lax.broadcasted_iota(jnp.int32, (ts, ts), 0)
    col = lax.broadcasted_iota(jnp.int32, (ts, ts), 1)
    s_n = jnp.where(col <= row, s_n, -jnp.inf)
    m = jnp.maximum(jnp.max(s_p, axis=-1, keepdims=True), jnp.max(s_n, axis=-1, keepdims=True))
    p_p = jnp.exp2(s_p - m)
    p_n = jnp.exp2(s_n - m)
    l = jnp.sum(p_p, axis=-1, keepdims=True) + jnp.sum(p_n, axis=-1, keepdims=True)
    acc = _dot(p_p.astype(BF16), vc_ref[...].astype(BF16)) + _dot(p_n.astype(BF16), vn_ref[...].astype(BF16))
    o_ref[...] = (acc / l).astype(o_ref.dtype)


def _decode_attention(qkv, k_cache, v_cache, c_past, c_new, heads):
    nb, ts, d3 = qkv.shape
    d = d3 // 3
    p = k_cache.shape[1]
    hd = d // heads
    return pl.pallas_call(
        _decode_kernel,
        grid=(nb, heads),
        in_specs=[pl.BlockSpec((None, ts, hd), lambda b, h: (b, 0, h)),
                  pl.BlockSpec((None, ts, hd), lambda b, h: (b, 0, heads + h)),
                  pl.BlockSpec((None, ts, hd), lambda b, h: (b, 0, 2 * heads + h)),
                  pl.BlockSpec((None, p, hd), lambda b, h: (b, 0, h)),
                  pl.BlockSpec((None, p, hd), lambda b, h: (b, 0, h)),
                  pl.BlockSpec((None, None, 1, p), lambda b, h: (b, h, 0, 0)),
                  pl.BlockSpec((None, None, 1, ts), lambda b, h: (b, h, 0, 0))],
        out_specs=pl.BlockSpec((None, ts, hd), lambda b, h: (b, 0, h)),
        out_shape=jax.ShapeDtypeStruct((nb, ts, d), BF16),
        compiler_params=_params("parallel", "parallel"),
        name="fox_decode",
    )(qkv, qkv, qkv, k_cache, v_cache, c_past, c_new)


def _gla_chunk(q, k, v, la2, st, chunk, sub):
    row = lax.broadcasted_iota(jnp.int32, q.shape, 0)
    b = la2
    s = 1
    while s < chunk:
        b = b + jnp.where(row >= s, pltpu.roll(b, s, axis=0), 0.0)
        s *= 2
    b_last = b[chunk - 1:chunk, :]

    o = _dot_nt((q * jnp.exp2(b)).astype(BF16), st.astype(BF16))

    lane = lax.broadcasted_iota(jnp.int32, (sub, chunk), 1)
    srow = lax.broadcasted_iota(jnp.int32, (sub, chunk), 0)
    blocks = []
    for i in range(chunk // sub):
        r0 = i * sub
        qi = q[r0:r0 + sub]
        bi = b[r0:r0 + sub]
        diag = jnp.zeros((sub, chunk), F32)
        for j in range(sub):
            kj = k[r0 + j:r0 + j + 1]
            bj = b[r0 + j:r0 + j + 1]
            col = jnp.sum(qi * kj * jnp.exp2(bi - bj), axis=-1, keepdims=True)
            diag = jnp.where(lane == r0 + j, col, diag)
        a_i = jnp.where(lane - r0 <= srow, diag, 0.0)
        if i > 0:
            ref_b = b[r0:r0 + 1]
            qg = (qi * jnp.exp2(bi - ref_b)).astype(BF16)
            kg = (k * jnp.exp2(jnp.minimum(ref_b - b, 0.0))).astype(BF16)
            a_i = jnp.where(lane < r0, _dot_nt(qg, kg), a_i)
        blocks.append(a_i)
    a = jnp.concatenate(blocks, axis=0)
    o = o + _dot(a.astype(BF16), v)

    kd = (k * jnp.exp2(b_last - b)).astype(BF16)
    st_new = st * jnp.exp2(b_last) + _dot_tn(v, kd)
    return o, st_new


def _gla_kernel(q_ref, k_ref, v_ref, la_ref, s0_ref, o_ref, s_out_ref, s_scr, *, chunk, sub, cps, nsteps, scale):
    step = pl.program_id(2)

    @pl.when(step == 0)
    def _():
        s_scr[...] = s0_ref[...]

    st = s_scr[...]
    for ci in range(cps):
        rows = slice(ci * chunk, (ci + 1) * chunk)
        o, st = _gla_chunk(q_ref[rows, :] * scale, k_ref[rows, :], v_ref[rows, :].astype(BF16),
                           la_ref[rows, :] * LOG2E, st, chunk, sub)
        o_ref[rows, :] = o
    s_scr[...] = st

    @pl.when(step == nsteps - 1)
    def _():
        s_out_ref[...] = st


def _gla(qkvr, la, s0t, heads, dk, dv, chunk):
    nb, nt, _ = qkvr.shape
    cps = GLA_CHUNKS_PER_STEP if nt % (GLA_CHUNKS_PER_STEP * chunk) == 0 else 1
    rows = cps * chunk
    nsteps = nt // rows
    kb = heads
    vb = 2 * heads * dk // dv
    assert (2 * heads * dk) % dv == 0
    return pl.pallas_call(
        functools.partial(_gla_kernel, chunk=chunk, sub=min(GLA_SUB, chunk), cps=cps, nsteps=nsteps, scale=dk ** -0.5),
        grid=(nb, heads, nsteps),
        in_specs=[pl.BlockSpec((None, rows, dk), lambda b, h, c: (b, c, h)),
                  pl.BlockSpec((None, rows, dk), lambda b, h, c: (b, c, kb + h)),
                  pl.BlockSpec((None, rows, dv), lambda b, h, c: (b, c, vb + h)),
                  pl.BlockSpec((None, rows, dk), lambda b, h, c: (b, c, h)),
                  pl.BlockSpec((None, None, dv, dk), lambda b, h, c: (b, h, 0, 0))],
        out_specs=[pl.BlockSpec((None, rows, dv), lambda b, h, c: (b, c, h)),
                   pl.BlockSpec((None, None, dv, dk), lambda b, h, c: (b, h, 0, 0))],
        out_shape=[jax.ShapeDtypeStruct((nb, nt, heads * dv), F32),
                   jax.ShapeDtypeStruct((nb, heads, dv, dk), F32)],
        scratch_shapes=[pltpu.VMEM((dv, dk), F32)],
        compiler_params=_params("parallel", "parallel", "arbitrary"),
        name="gla_chunk",
    )(qkvr, qkvr, qkvr, la, s0t)


def _gla_out_kernel(o_ref, r_ref, g_ref, a_ref, *, heads, dv):
    g = g_ref[...]
    for h in range(heads):
        sl = slice(h * dv, (h + 1) * dv)
        x = o_ref[:, sl]
        r = r_ref[:, sl]
        ms = jnp.mean(x * x, axis=-1, keepdims=True)
        a_ref[:, sl] = (x * lax.rsqrt(ms + EPS) * g * (r * jax.nn.sigmoid(r))).astype(a_ref.dtype)


def _gla_out(o, qkvr, g_o, heads, dv):
    m, d = o.shape
    tm = _tile(m, 512)
    rb = qkvr.shape[1] // d - 1
    return pl.pallas_call(
        functools.partial(_gla_out_kernel, heads=heads, dv=dv),
        grid=(m // tm,),
        in_specs=[pl.BlockSpec((tm, d), lambda i: (i, 0)),
                  pl.BlockSpec((tm, d), lambda i: (i, rb)),
                  pl.BlockSpec((1, dv), lambda i: (0, 0))],
        out_specs=pl.BlockSpec((tm, d), lambda i: (i, 0)),
        out_shape=jax.ShapeDtypeStruct((m, d), BF16),
        compiler_params=_params("parallel"),
        name="gla_out_norm",
    )(o, qkvr, g_o.reshape(1, dv))


def _conv_layer(x, nb, nt, hist, g_mix, w_in, w_dw, g_norm, w_out):
    m, d = x.shape
    width = w_dw.shape[0]
    u = _conv_in(x, g_mix, w_in)
    u3 = u.reshape(nb, nt, d)
    if hist is None:
        hist_pad = jnp.zeros((nb, CONV_HALO, d), F32)
        tail = u3[:, nt - (width - 1):]
    else:
        hist_pad = jnp.pad(hist, ((0, 0), (CONV_HALO - (width - 1), 0), (0, 0)))
        tail = jnp.concatenate([hist, u3], axis=1)[:, nt:]
    a = _conv_norm_silu(u3, hist_pad, w_dw, g_norm)
    return _out_proj(a.reshape(m, d), w_out, x), tail


def _pad_cols(w, n):
    return jnp.pad(w, ((0, 0), (0, n - w.shape[1])))


def _fox_layer(xp, xs, dims, cache_k, cache_v, cache_logf, g_mix, w_qkv, w_f, b_f, g_q, g_k, w_o):
    nb, nt, nbs, nts = dims
    d = xp.shape[1]
    heads, hd = cache_k.shape[2], cache_k.shape[3]
    past = cache_k.shape[1]
    hb = 2 if heads % 2 == 0 else 1
    w_qkv = w_qkv.astype(BF16)
    w_o = w_o.astype(BF16)
    w_fp = _pad_cols(w_f, LANES).astype(BF16)
    b_fp = jnp.pad(b_f, (0, LANES - heads)).reshape(1, LANES)
    gain_row = jnp.concatenate([jnp.tile(g_q, heads) * (hd ** -0.5 * LOG2E), jnp.tile(g_k, heads),
                                jnp.ones((d,), F32)]).reshape(1, 3 * d)

    def split_new(qkv, lf, b_, t_):
        k = qkv[:, d:2 * d].reshape(b_, t_, heads, hd)
        v = qkv[:, 2 * d:].reshape(b_, t_, heads, hd)
        return k, v, lf[:, :heads].reshape(b_, t_, heads)

    qkv, lf = _fox_proj(xp, g_mix, w_qkv, gain_row, w_fp, b_fp, hd)
    new_p = split_new(qkv, lf, nb, nt)
    c = _cumsum_lanes(jnp.transpose(new_p[2], (0, 2, 1)))
    o = _flash_attention(qkv.reshape(nb, nt, 3 * d), c.reshape(nb, heads // hb, hb, nt), heads, hb)
    yp = _out_proj(o.reshape(nb * nt, d), w_o, xp)

    qkv_s, lf_s = _fox_proj(xs, g_mix, w_qkv, gain_row, w_fp, b_fp, hd)
    new_s = split_new(qkv_s, lf_s, nbs, nts)
    c_past = _cumsum_lanes(jnp.transpose(cache_logf, (0, 2, 1)), sub_last=True)
    nts_pad = -(-nts // LANES) * LANES
    lfs_t = jnp.pad(jnp.transpose(new_s[2], (0, 2, 1)), ((0, 0), (0, 0), (0, nts_pad - nts)))
    c_new = _cumsum_lanes(lfs_t)[:, :, :nts]
    os_ = _decode_attention(qkv_s.reshape(nbs, nts, 3 * d), cache_k.reshape(nbs, past, d), cache_v.reshape(nbs, past, d),
                            c_past.reshape(nbs, heads, 1, past), c_new.reshape(nbs, heads, 1, nts), heads)
    ys = _out_proj(os_.reshape(nbs * nts, d), w_o, xs)
    return yp, ys, new_p + new_s


def _gla_group(x, nb, nt, s0t, chunk, heads, dk, dv, g_mix, w_qkvr, w_a1p, w_a2p, b_a, g_o, w_o):
    m, d = x.shape
    qkvr, la = _gla_proj(x, g_mix, w_qkvr, w_a1p, w_a2p, b_a)
    n = qkvr.shape[1]
    o, st = _gla(qkvr.reshape(nb, nt, n), la.reshape(nb, nt, heads * dk), s0t, heads, dk, dv, chunk)
    a = _gla_out(o.reshape(m, d), qkvr, g_o, heads, dv)
    return _out_proj(a, w_o, x), jnp.swapaxes(st, 2, 3)


def _gla_layer(xp, xs, dims, state, g_mix, w_qkvr, w_a1, w_a2, b_a, g_o, w_o):
    nb, nt, nbs, nts = dims
    heads, dk, dv = state.shape[1], state.shape[2], state.shape[3]
    rank = w_a1.shape[1]
    w_qkvr = w_qkvr.astype(BF16)
    w_o = w_o.astype(BF16)
    w_a1p = _pad_cols(w_a1, LANES).astype(BF16)
    w_a2p = jnp.pad(w_a2, ((0, LANES - rank), (0, 0))).astype(BF16)
    common = (heads, dk, dv, g_mix, w_qkvr, w_a1p, w_a2p, b_a, g_o, w_o)
    zeros = jnp.zeros((nb, heads, dv, dk), F32)
    yp, sp = _gla_group(xp, nb, nt, zeros, GLA_CHUNK, *common)
    ys, ss = _gla_group(xs, nbs, nts, jnp.swapaxes(state, 2, 3), nts, *common)
    return yp, ys, (sp, ss)


def kernel(x_prompt, x_sample, cache_conv_l0, cache_k_l1, cache_v_l1, cache_logf_l1, state_gla_l2, cache_conv_l3, norm_mix_l0, conv_w_in_l0, conv_w_dw_l0, conv_norm_l0, conv_w_out_l0, norm_ffn_l0, ffn_w_up_l0, ffn_w_down_l0, norm_mix_l1, fox_w_qkv_l1, fox_w_f_l1, fox_b_f_l1, fox_q_norm_l1, fox_k_norm_l1, fox_w_o_l1, norm_ffn_l1, ffn_w_up_l1, ffn_w_down_l1, norm_mix_l2, gla_w_qkvr_l2, gla_w_a1_l2, gla_w_a2_l2, gla_b_a_l2, gla_o_norm_l2, gla_w_o_l2, norm_ffn_l2, ffn_w_up_l2, ffn_w_down_l2, norm_mix_l3, conv_w_in_l3, conv_w_dw_l3, conv_norm_l3, conv_w_out_l3, norm_ffn_l3, ffn_w_up_l3, ffn_w_down_l3):
    nb, nt, d = x_prompt.shape
    nbs, nts, _ = x_sample.shape
    dims = (nb, nt, nbs, nts)
    xp = x_prompt.reshape(nb * nt, d)
    xs = x_sample.reshape(nbs * nts, d)

    def ffn(xp, xs, g, wu, wd):
        wu = wu.astype(BF16)
        wd = wd.astype(BF16)
        return _ffn(xp, g, wu, wd), _ffn(xs, g, wu, wd)

    def conv(xp, xs, cache, g_mix, w_in, w_dw, g_norm, w_out):
        w_in = w_in.astype(BF16)
        w_out = w_out.astype(BF16)
        yp, sp = _conv_layer(xp, nb, nt, None, g_mix, w_in, w_dw, g_norm, w_out)
        ys, ss = _conv_layer(xs, nbs, nts, cache, g_mix, w_in, w_dw, g_norm, w_out)
        return yp, ys, (sp, ss)

    xp, xs, st0 = conv(xp, xs, cache_conv_l0, norm_mix_l0, conv_w_in_l0, conv_w_dw_l0, conv_norm_l0, conv_w_out_l0)
    xp, xs = ffn(xp, xs, norm_ffn_l0, ffn_w_up_l0, ffn_w_down_l0)
    xp, xs, st1 = _fox_layer(xp, xs, dims, cache_k_l1, cache_v_l1, cache_logf_l1, norm_mix_l1, fox_w_qkv_l1,
                             fox_w_f_l1, fox_b_f_l1, fox_q_norm_l1, fox_k_norm_l1, fox_w_o_l1)
    xp, xs = ffn(xp, xs, norm_ffn_l1, ffn_w_up_l1, ffn_w_down_l1)
    xp, xs, st2 = _gla_layer(xp, xs, dims, state_gla_l2, norm_mix_l2, gla_w_qkvr_l2, gla_w_a1_l2, gla_w_a2_l2,
                             gla_b_a_l2, gla_o_norm_l2, gla_w_o_l2)
    xp, xs = ffn(xp, xs, norm_ffn_l2, ffn_w_up_l2, ffn_w_down_l2)
    xp, xs, st3 = conv(xp, xs, cache_conv_l3, norm_mix_l3, conv_w_in_l3, conv_w_dw_l3, conv_norm_l3, conv_w_out_l3)
    xp, xs = ffn(xp, xs, norm_ffn_l3, ffn_w_up_l3, ffn_w_down_l3)
    return (xp.reshape(nb, nt, d), xs.reshape(nbs, nts, d), *st0, *st1, *st2, *st3)
```

```python
import functools
import math

import jax
import jax.numpy as jnp
from jax import lax
from jax.experimental import pallas as pl
from jax.experimental.pallas import tpu as pltpu

F32 = jnp.float32
BF16 = jnp.bfloat16

EPS = 1e-6
LOG2E = math.log2(math.e)
GLA_CHUNK = 64
GLA_SUB = 16
GLA_CHUNKS_PER_STEP = 2
GLA_TAU = 16.0
FLASH_ROW_CHUNK = 32
FLASH_HEADS_PER_STEP = 4
LANES = 128
SUBLANES = 8
CONV_HALO = 32
VMEM_LIMIT_BYTES = 56 * 1024 * 1024


def _tile(n, pref):
    if n <= pref:
        return n
    t = pref - pref % 8
    while t >= 8:
        if n % t == 0:
            return t
        t -= 8
    return n


def _params(*sem):
    return pltpu.CompilerParams(dimension_semantics=sem, vmem_limit_bytes=VMEM_LIMIT_BYTES)


def _log_sigmoid(x):
    return jnp.minimum(x, 0.0) - jnp.log(1.0 + jnp.exp(-jnp.abs(x)))


def _rms_bf16(x, g):
    ms = jnp.mean(x * x, axis=-1, keepdims=True)
    return (x * lax.rsqrt(ms + EPS) * g).astype(BF16)


def _dot(a, b):
    return jnp.dot(a, b, preferred_element_type=F32)


def _dot_nt(a, b):
    return lax.dot_general(a, b, (((1,), (1,)), ((), ())), preferred_element_type=F32)


def _dot_tn(a, b):
    return lax.dot_general(a, b, (((0,), (0,)), ((), ())), preferred_element_type=F32)


def _conv_in_kernel(x_ref, g_ref, wa_ref, wb_ref, o_ref, h_scr):
    @pl.when(pl.program_id(1) == 0)
    def _():
        h_scr[...] = _rms_bf16(x_ref[...], g_ref[...])

    h = h_scr[...]
    o_ref[...] = _dot(h, wa_ref[...]) * jax.nn.sigmoid(_dot(h, wb_ref[...]))


def _conv_in(x, g, w_in):
    m, d = x.shape
    n = w_in.shape[1] // 2
    tm = _tile(m, 1024)
    tn = _tile(n, 512)
    nb = n // tn
    return pl.pallas_call(
        _conv_in_kernel,
        grid=(m // tm, nb),
        in_specs=[pl.BlockSpec((tm, d), lambda i, j: (i, 0)),
                  pl.BlockSpec((1, d), lambda i, j: (0, 0)),
                  pl.BlockSpec((d, tn), lambda i, j: (0, j)),
                  pl.BlockSpec((d, tn), lambda i, j: (0, j + nb))],
        out_specs=pl.BlockSpec((tm, tn), lambda i, j: (i, j)),
        out_shape=jax.ShapeDtypeStruct((m, n), F32),
        scratch_shapes=[pltpu.VMEM((tm, d), BF16)],
        compiler_params=_params("parallel", "arbitrary"),
        name="conv_in_glu",
    )(x, g.reshape(1, d), w_in, w_in)


def _fox_proj_kernel(x_ref, g_ref, w_ref, gain_ref, wf_ref, bf_ref, q_ref, k_ref, v_ref, lf_ref, h_scr, *,
                     nq, heads_per_tile, head_dim):
    j = pl.program_id(1)

    @pl.when(j == 0)
    def _():
        h = _rms_bf16(x_ref[...], g_ref[...])
        h_scr[...] = h
        lf_ref[...] = _log_sigmoid(_dot(h, wf_ref[...]) + bf_ref[...])

    acc = _dot(h_scr[...], w_ref[...])

    def head_norm(dst_ref):
        gain = gain_ref[...]
        for h in range(heads_per_tile):
            sl = slice(h * head_dim, (h + 1) * head_dim)
            x = acc[:, sl]
            ms = jnp.mean(x * x, axis=-1, keepdims=True)
            dst_ref[:, sl] = (x * lax.rsqrt(ms + EPS) * gain[:, sl]).astype(dst_ref.dtype)

    @pl.when(j < nq)
    def _():
        head_norm(q_ref)

    @pl.when((j >= nq) & (j < 2 * nq))
    def _():
        head_norm(k_ref)

    @pl.when(j >= 2 * nq)
    def _():
        v_ref[...] = acc


def _fox_proj(x, g, w_qkv, gain_row, w_fp, b_fp, head_dim):
    m, d = x.shape
    n = w_qkv.shape[1]
    tm = _tile(m, 1024)
    tn = _tile(d, 512)
    nq = d // tn
    assert tn % head_dim == 0 and n == 3 * d

    def out_cols(first):
        return lambda i, j: (i, jnp.clip(j - first, 0, nq - 1))

    return pl.pallas_call(
        functools.partial(_fox_proj_kernel, nq=nq, heads_per_tile=tn // head_dim, head_dim=head_dim),
        grid=(m // tm, n // tn),
        in_specs=[pl.BlockSpec((tm, d), lambda i, j: (i, 0)),
                  pl.BlockSpec((1, d), lambda i, j: (0, 0)),
                  pl.BlockSpec((d, tn), lambda i, j: (0, j)),
                  pl.BlockSpec((1, tn), lambda i, j: (0, j)),
                  pl.BlockSpec((d, LANES), lambda i, j: (0, 0)),
                  pl.BlockSpec((1, LANES), lambda i, j: (0, 0))],
        out_specs=[pl.BlockSpec((tm, tn), out_cols(0)),
                   pl.BlockSpec((tm, tn), out_cols(nq)),
                   pl.BlockSpec((tm, tn), out_cols(2 * nq)),
                   pl.BlockSpec((tm, LANES), lambda i, j: (i, 0))],
        out_shape=[jax.ShapeDtypeStruct((m, d), BF16), jax.ShapeDtypeStruct((m, d), F32),
                   jax.ShapeDtypeStruct((m, d), F32), jax.ShapeDtypeStruct((m, LANES), F32)],
        scratch_shapes=[pltpu.VMEM((tm, d), BF16)],
        compiler_params=_params("parallel", "arbitrary"),
        name="fox_proj",
    )(x, g.reshape(1, d), w_qkv, gain_row, w_fp, b_fp)


def _gla_proj_kernel(x_ref, g_ref, w_ref, w1_ref, w2_ref, ba_ref, o_ref, la_ref, h_scr):
    @pl.when(pl.program_id(1) == 0)
    def _():
        h = _rms_bf16(x_ref[...], g_ref[...])
        h_scr[...] = h
        low = _dot(h, w1_ref[...]).astype(BF16)
        la_ref[...] = _log_sigmoid(_dot(low, w2_ref[...]) + ba_ref[...]) * (1.0 / GLA_TAU)

    o_ref[...] = _dot(h_scr[...], w_ref[...])


def _gla_proj(x, g, w_qkvr, w_a1p, w_a2p, b_a):
    m, d = x.shape
    n = w_qkvr.shape[1]
    na = w_a2p.shape[1]
    tm = _tile(m, 1024)
    tn = _tile(n, 1024)
    return pl.pallas_call(
        _gla_proj_kernel,
        grid=(m // tm, n // tn),
        in_specs=[pl.BlockSpec((tm, d), lambda i, j: (i, 0)),
                  pl.BlockSpec((1, d), lambda i, j: (0, 0)),
                  pl.BlockSpec((d, tn), lambda i, j: (0, j)),
                  pl.BlockSpec(w_a1p.shape, lambda i, j: (0, 0)),
                  pl.BlockSpec(w_a2p.shape, lambda i, j: (0, 0)),
                  pl.BlockSpec((1, na), lambda i, j: (0, 0))],
        out_specs=[pl.BlockSpec((tm, tn), lambda i, j: (i, j)),
                   pl.BlockSpec((tm, na), lambda i, j: (i, 0))],
        out_shape=[jax.ShapeDtypeStruct((m, n), F32), jax.ShapeDtypeStruct((m, na), F32)],
        scratch_shapes=[pltpu.VMEM((tm, d), BF16)],
        compiler_params=_params("parallel", "arbitrary"),
        name="gla_proj",
    )(x, g.reshape(1, d), w_qkvr, w_a1p, w_a2p, b_a.reshape(1, na))


def _out_proj_kernel(a_ref, w_ref, r_ref, o_ref):
    o_ref[...] = r_ref[...] + _dot(a_ref[...], w_ref[...])


def _out_proj(a, w, res):
    m, k = a.shape
    n = w.shape[1]
    tm = _tile(m, 1024)
    tn = _tile(n, 1024)
    return pl.pallas_call(
        _out_proj_kernel,
        grid=(m // tm, n // tn),
        in_specs=[pl.BlockSpec((tm, k), lambda i, j: (i, 0)),
                  pl.BlockSpec((k, tn), lambda i, j: (0, j)),
                  pl.BlockSpec((tm, tn), lambda i, j: (i, j))],
        out_specs=pl.BlockSpec((tm, tn), lambda i, j: (i, j)),
        out_shape=jax.ShapeDtypeStruct((m, n), F32),
        compiler_params=_params("parallel", "arbitrary"),
        name="out_proj",
    )(a, w, res)


def _ffn_kernel(x_ref, g_ref, wu_ref, wd_ref, o_ref, h_scr):
    @pl.when(pl.program_id(1) == 0)
    def _():
        x = x_ref[...]
        h_scr[...] = _rms_bf16(x, g_ref[...])
        o_ref[...] = x

    u = jnp.maximum(_dot(h_scr[...], wu_ref[...]), 0.0)
    u = (u * u).astype(BF16)
    d = o_ref.shape[1]
    dc = min(d, 512)
    for c in range(d // dc):
        cs = slice(c * dc, (c + 1) * dc)
        o_ref[:, cs] += _dot(u, wd_ref[:, cs])


def _ffn(x, g, wu, wd):
    m, d = x.shape
    f = wu.shape[1]
    tm = _tile(m, 1024)
    tf = _tile(f, 512)
    return pl.pallas_call(
        _ffn_kernel,
        grid=(m // tm, f // tf),
        in_specs=[pl.BlockSpec((tm, d), lambda i, j: (i, 0)),
                  pl.BlockSpec((1, d), lambda i, j: (0, 0)),
                  pl.BlockSpec((d, tf), lambda i, j: (0, j)),
                  pl.BlockSpec((tf, d), lambda i, j: (j, 0))],
        out_specs=pl.BlockSpec((tm, d), lambda i, j: (i, 0)),
        out_shape=jax.ShapeDtypeStruct((m, d), F32),
        scratch_shapes=[pltpu.VMEM((tm, d), BF16)],
        compiler_params=_params("parallel", "arbitrary"),
        name="ffn",
    )(x, g.reshape(1, d), wu, wd)


def _shift_rows_up(x, r):
    n, c = x.shape
    x3 = x.reshape(n // SUBLANES, SUBLANES, c)
    rot = pltpu.roll(x3, SUBLANES - r, axis=1)
    sub = lax.broadcasted_iota(jnp.int32, (n // SUBLANES - 1, SUBLANES, c), 1)
    return jnp.where(sub < SUBLANES - r, rot[:-1], rot[1:]).reshape(n - SUBLANES, c)


def _conv_kernel(u_ref, up_ref, hist_ref, w_ref, g_ref, o_ref, ext_scr, y_scr, *, tm, width, rb, lc):
    i = pl.program_id(1)
    d = u_ref.shape[-1]
    nr = 2 * SUBLANES

    @pl.when(i == 0)
    def _():
        ext_scr[0:CONV_HALO, :] = hist_ref[...]

    @pl.when(i > 0)
    def _():
        ext_scr[0:CONV_HALO, :] = up_ref[...]

    ext_scr[CONV_HALO:CONV_HALO + tm, :] = u_ref[...]
    off = CONV_HALO - (width - 1)

    def row_block(r, carry):
        r0 = pl.multiple_of(r * rb, rb)
        for c in range(d // lc):
            cs = slice(c * lc, (c + 1) * lc)
            acc = jnp.zeros((rb, lc), F32)
            for rem in range(SUBLANES):
                part = None
                for k in range(width):
                    if (off + k) % SUBLANES != rem:
                        continue
                    base = (off + k) // SUBLANES * SUBLANES
                    rows = rb + SUBLANES if rem else rb
                    win = ext_scr[pl.ds(r0 + base, rows), cs].reshape(rows // SUBLANES, SUBLANES, lc)
                    term = (win * w_ref[k * SUBLANES:(k + 1) * SUBLANES, cs][None]).reshape(rows, lc)
                    part = term if part is None else part + term
                if part is None:
                    continue
                acc = acc + (_shift_rows_up(part, rem) if rem else part)
            y_scr[pl.ds(r0, rb), cs] = acc
        for s in range(rb // nr):
            rows = pl.ds(r0 + s * nr, nr)
            y = y_scr[rows, :]
            ms = jnp.mean(y * y, axis=-1, keepdims=True)
            z = y * lax.rsqrt(ms + EPS) * g_ref[...]
            o_ref[rows, :] = (z * jax.nn.sigmoid(z)).astype(o_ref.dtype)
        return carry

    lax.fori_loop(0, tm // rb, row_block, 0)


def _conv_norm_silu(u, hist_pad, w_dw, g):
    nb, nt, d = u.shape
    width = w_dw.shape[0]
    assert width - 1 <= CONV_HALO
    tm = _tile(nt, 512)
    assert tm % CONV_HALO == 0
    rb = 64 if tm % 64 == 0 else CONV_HALO
    lc = min(256, d)
    hb = tm // CONV_HALO
    w_rep = jnp.repeat(w_dw, SUBLANES, axis=0)
    return pl.pallas_call(
        functools.partial(_conv_kernel, tm=tm, width=width, rb=rb, lc=lc),
        grid=(nb, nt // tm),
        in_specs=[pl.BlockSpec((None, tm, d), lambda b, i: (b, i, 0)),
                  pl.BlockSpec((None, CONV_HALO, d), lambda b, i: (b, jnp.maximum(i * hb - 1, 0), 0)),
                  pl.BlockSpec((None, CONV_HALO, d), lambda b, i: (b, 0, 0)),
                  pl.BlockSpec((width * SUBLANES, d), lambda b, i: (0, 0)),
                  pl.BlockSpec((1, d), lambda b, i: (0, 0))],
        out_specs=pl.BlockSpec((None, tm, d), lambda b, i: (b, i, 0)),
        out_shape=jax.ShapeDtypeStruct((nb, nt, d), BF16),
        scratch_shapes=[pltpu.VMEM((CONV_HALO + tm, d), F32), pltpu.VMEM((tm, d), F32)],
        compiler_params=_params("parallel", "arbitrary"),
        name="conv_norm_silu",
    )(u, u, hist_pad, w_rep, g.reshape(1, d))


def _cumsum_kernel(x_ref, o_ref, *, n, sub_last, scale):
    x = x_ref[...]
    lane = lax.broadcasted_iota(jnp.int32, x.shape, 1)
    s = 1
    while s < n:
        x = x + jnp.where(lane >= s, pltpu.roll(x, s, axis=1), 0.0)
        s *= 2
    if sub_last:
        x = x - x[:, n - 1:n]
    o_ref[...] = x * scale


def _cumsum_lanes(x, *, sub_last=False, scale=LOG2E):
    nb, h, n = x.shape
    return pl.pallas_call(
        functools.partial(_cumsum_kernel, n=n, sub_last=sub_last, scale=scale),
        grid=(nb,),
        in_specs=[pl.BlockSpec((None, h, n), lambda b: (b, 0, 0))],
        out_specs=pl.BlockSpec((None, h, n), lambda b: (b, 0, 0)),
        out_shape=jax.ShapeDtypeStruct((nb, h, n), F32),
        compiler_params=_params("parallel"),
        name="cumsum_logf",
    )(x)


def _flash_kernel(q_ref, k_ref, v_ref, c_ref, o_ref, k_scr, v_scr, s_scr, p_scr, m_scr, l_scr, acc_scr, *,
                  tq, hd, hb, rc):
    qi = pl.program_id(2)

    @pl.when(qi == 0)
    def _():
        k_scr[...] = k_ref[...].astype(BF16)
        v_scr[...] = v_ref[...].astype(BF16)

    heads = [slice(h * hd, (h + 1) * hd) for h in range(hb)]
    qs = [q_ref[:, hs] for hs in heads]
    m_scr[...] = jnp.full(m_scr.shape, -jnp.inf, F32)
    l_scr[...] = jnp.zeros(l_scr.shape, F32)
    acc_scr[...] = jnp.zeros(acc_scr.shape, F32)

    def step(kj, masked):
        k0 = pl.multiple_of(kj * tq, tq)
        for h, hs in enumerate(heads):
            s_scr[h] = _dot_nt(qs[h], k_scr[pl.ds(k0, tq), hs])
            c_row = c_ref[h:h + 1, pl.ds(k0, tq)]
            m_all = m_scr[h]
            l_all = l_scr[h]
            m_out, l_out, a_out = [], [], []
            for r in range(tq // rc):
                rows = slice(r * rc, (r + 1) * rc)
                blocks = []
                for i in range(tq // LANES):
                    cols = slice(i * LANES, (i + 1) * LANES)
                    blk = s_scr[h, rows, cols] - c_row[:, cols]
                    if masked:
                        row = lax.broadcasted_iota(jnp.int32, (rc, LANES), 0) + r * rc
                        col = lax.broadcasted_iota(jnp.int32, (rc, LANES), 1) + i * LANES
                        blk = jnp.where(col <= row, blk, -jnp.inf)
                    blocks.append(blk)
                m_old = m_all[rows]
                m_new = jnp.maximum(m_old, jnp.max(functools.reduce(jnp.maximum, blocks), axis=-1, keepdims=True))
                alpha = jnp.exp2(m_old - m_new)
                ps = [jnp.exp2(blk - m_new) for blk in blocks]
                l_out.append(alpha * l_all[rows] + jnp.sum(functools.reduce(jnp.add, ps), axis=-1, keepdims=True))
                m_out.append(m_new)
                a_out.append(alpha)
                for i, pb in enumerate(ps):
                    p_scr[h, rows, i * LANES:(i + 1) * LANES] = pb.astype(BF16)
            m_scr[h] = jnp.concatenate(m_out, axis=0)
            l_scr[h] = jnp.concatenate(l_out, axis=0)
            alpha = jnp.concatenate(a_out, axis=0)
            acc_scr[h] = alpha * acc_scr[h] + _dot(p_scr[h], v_scr[pl.ds(k0, tq), hs])

    def body(kj, carry):
        step(kj, False)
        return carry

    lax.fori_loop(0, qi, body, 0)
    step(qi, True)
    for h, hs in enumerate(heads):
        o_ref[:, hs] = (acc_scr[h] / l_scr[h]).astype(o_ref.dtype)


def _flash_attention(q, k, v, c, heads, hb):
    nb, nt, d = q.shape
    hd = d // heads
    assert hd == LANES
    tq = _tile(nt, 512)
    w = hb * hd
    nhb = heads // hb
    return pl.pallas_call(
        functools.partial(_flash_kernel, tq=tq, hd=hd, hb=hb, rc=min(FLASH_ROW_CHUNK, tq)),
        grid=(nb, nhb, nt // tq),
        in_specs=[pl.BlockSpec((None, tq, w), lambda b, h, i: (b, i, h)),
                  pl.BlockSpec((None, nt, w), lambda b, h, i: (b, 0, h)),
                  pl.BlockSpec((None, nt, w), lambda b, h, i: (b, 0, h)),
                  pl.BlockSpec((None, None, hb, nt), lambda b, h, i: (b, h, 0, 0))],
        out_specs=pl.BlockSpec((None, tq, w), lambda b, h, i: (b, i, h)),
        out_shape=jax.ShapeDtypeStruct((nb, nt, d), BF16),
        scratch_shapes=[pltpu.VMEM((nt, w), BF16), pltpu.VMEM((nt, w), BF16),
                        pltpu.VMEM((hb, tq, tq), F32), pltpu.VMEM((hb, tq, tq), BF16),
                        pltpu.VMEM((hb, tq, LANES), F32), pltpu.VMEM((hb, tq, LANES), F32),
                        pltpu.VMEM((hb, tq, hd), F32)],
        compiler_params=_params("parallel", "parallel", "arbitrary"),
        name="fox_flash",
    )(q, k, v, c)


def _decode_kernel(q_ref, kn_ref, vn_ref, kc_ref, vc_ref, cp_ref, cn_ref, o_ref):
    q = q_ref[...]
    ts = q.shape[0]
    s_p = _dot_nt(q, kc_ref[...].astype(BF16)) - cp_ref[...]
    s_n = _dot_nt(q, kn_ref[...].astype(BF16)) - cn_ref[...]
    row = lax.broadcasted_iota(jnp.int32, (ts, ts), 0)
    col = lax.broadcasted_iota(jnp.int32, (ts, ts), 1)
    s_n = jnp.where(col <= row, s_n, -jnp.inf)
    m = jnp.maximum(jnp.max(s_p, axis=-1, keepdims=True), jnp.max(s_n, axis=-1, keepdims=True))
    p_p = jnp.exp2(s_p - m)
    p_n = jnp.exp2(s_n - m)
    l = jnp.sum(p_p, axis=-1, keepdims=True) + jnp.sum(p_n, axis=-1, keepdims=True)
    acc = _dot(p_p.astype(BF16), vc_ref[...].astype(BF16)) + _dot(p_n.astype(BF16), vn_ref[...].astype(BF16))
    o_ref[...] = (acc / l).astype(o_ref.dtype)


def _decode_attention(q, k_new, v_new, k_cache, v_cache, c_past, c_new, heads):
    nb, ts, d = q.shape
    p = k_cache.shape[1]
    hd = d // heads
    return pl.pallas_call(
        _decode_kernel,
        grid=(nb, heads),
        in_specs=[pl.BlockSpec((None, ts, hd), lambda b, h: (b, 0, h)),
                  pl.BlockSpec((None, ts, hd), lambda b, h: (b, 0, h)),
                  pl.BlockSpec((None, ts, hd), lambda b, h: (b, 0, h)),
                  pl.BlockSpec((None, p, hd), lambda b, h: (b, 0, h)),
                  pl.BlockSpec((None, p, hd), lambda b, h: (b, 0, h)),
                  pl.BlockSpec((None, None, 1, p), lambda b, h: (b, h, 0, 0)),
                  pl.BlockSpec((None, None, 1, ts), lambda b, h: (b, h, 0, 0))],
        out_specs=pl.BlockSpec((None, ts, hd), lambda b, h: (b, 0, h)),
        out_shape=jax.ShapeDtypeStruct((nb, ts, d), BF16),
        compiler_params=_params("parallel", "parallel"),
        name="fox_decode",
    )(q, k_new, v_new, k_cache, v_cache, c_past, c_new)


def _gla_chunk(q, k, v, la2, st, chunk, sub):
    row = lax.broadcasted_iota(jnp.int32, q.shape, 0)
    b = la2
    s = 1
    while s < chunk:
        b = b + jnp.where(row >= s, pltpu.roll(b, s, axis=0), 0.0)
        s *= 2
    b_last = b[chunk - 1:chunk, :]

    o = _dot_nt((q * jnp.exp2(b)).astype(BF16), st.astype(BF16))

    lane = lax.broadcasted_iota(jnp.int32, (sub, chunk), 1)
    srow = lax.broadcasted_iota(jnp.int32, (sub, chunk), 0)
    blocks = []
    for i in range(chunk // sub):
        r0 = i * sub
        qi = q[r0:r0 + sub]
        bi = b[r0:r0 + sub]
        diag = jnp.zeros((sub, chunk), F32)
        for j in range(sub):
            kj = k[r0 + j:r0 + j + 1]
            bj = b[r0 + j:r0 + j + 1]
            col = jnp.sum(qi * kj * jnp.exp2(bi - bj), axis=-1, keepdims=True)
            diag = jnp.where(lane == r0 + j, col, diag)
        a_i = jnp.where(lane - r0 <= srow, diag, 0.0)
        if i > 0:
            ref_b = b[r0:r0 + 1]
            qg = (qi * jnp.exp2(bi - ref_b)).astype(BF16)
            kg = (k * jnp.exp2(jnp.minimum(ref_b - b, 0.0))).astype(BF16)
            a_i = jnp.where(lane < r0, _dot_nt(qg, kg), a_i)
        blocks.append(a_i)
    a = jnp.concatenate(blocks, axis=0)
    o = o + _dot(a.astype(BF16), v)

    kd = (k * jnp.exp2(b_last - b)).astype(BF16)
    st_new = st * jnp.exp2(b_last) + _dot_tn(v, kd)
    return o, st_new


def _gla_kernel(q_ref, k_ref, v_ref, la_ref, s0_ref, o_ref, s_out_ref, s_scr, *, chunk, sub, cps, nsteps, scale):
    step = pl.program_id(2)

    @pl.when(step == 0)
    def _():
        s_scr[...] = s0_ref[...]

    st = s_scr[...]
    for ci in range(cps):
        rows = slice(ci * chunk, (ci + 1) * chunk)
        o, st = _gla_chunk(q_ref[rows, :] * scale, k_ref[rows, :], v_ref[rows, :].astype(BF16),
                           la_ref[rows, :] * LOG2E, st, chunk, sub)
        o_ref[rows, :] = o
    s_scr[...] = st

    @pl.when(step == nsteps - 1)
    def _():
        s_out_ref[...] = st


def _gla(qkvr, la, s0t, heads, dk, dv, chunk):
    nb, nt, _ = qkvr.shape
    cps = GLA_CHUNKS_PER_STEP if nt % (GLA_CHUNKS_PER_STEP * chunk) == 0 else 1
    rows = cps * chunk
    nsteps = nt // rows
    kb = heads
    vb = 2 * heads * dk // dv
    assert (2 * heads * dk) % dv == 0
    return pl.pallas_call(
        functools.partial(_gla_kernel, chunk=chunk, sub=min(GLA_SUB, chunk), cps=cps, nsteps=nsteps, scale=dk ** -0.5),
        grid=(nb, heads, nsteps),
        in_specs=[pl.BlockSpec((None, rows, dk), lambda b, h, c: (b, c, h)),
                  pl.BlockSpec((None, rows, dk), lambda b, h, c: (b, c, kb + h)),
                  pl.BlockSpec((None, rows, dv), lambda b, h, c: (b, c, vb + h)),
                  pl.BlockSpec((None, rows, dk), lambda b, h, c: (b, c, h)),
                  pl.BlockSpec((None, None, dv, dk), lambda b, h, c: (b, h, 0, 0))],
        out_specs=[pl.BlockSpec((None, rows, dv), lambda b, h, c: (b, c, h)),
                   pl.BlockSpec((None, None, dv, dk), lambda b, h, c: (b, h, 0, 0))],
        out_shape=[jax.ShapeDtypeStruct((nb, nt, heads * dv), F32),
                   jax.ShapeDtypeStruct((nb, heads, dv, dk), F32)],
        scratch_shapes=[pltpu.VMEM((dv, dk), F32)],
        compiler_params=_params("parallel", "parallel", "arbitrary"),
        name="gla_chunk",
    )(qkvr, qkvr, qkvr, la, s0t)


def _gla_out_kernel(o_ref, r_ref, g_ref, a_ref, *, heads, dv):
    g = g_ref[...]
    for h in range(heads):
        sl = slice(h * dv, (h + 1) * dv)
        x = o_ref[:, sl]
        r = r_ref[:, sl]
        ms = jnp.mean(x * x, axis=-1, keepdims=True)
        a_ref[:, sl] = (x * lax.rsqrt(ms + EPS) * g * (r * jax.nn.sigmoid(r))).astype(a_ref.dtype)


def _gla_out(o, qkvr, g_o, heads, dv):
    m, d = o.shape
    tm = _tile(m, 512)
    rb = qkvr.shape[1] // d - 1
    return pl.pallas_call(
        functools.partial(_gla_out_kernel, heads=heads, dv=dv),
        grid=(m // tm,),
        in_specs=[pl.BlockSpec((tm, d), lambda i: (i, 0)),
                  pl.BlockSpec((tm, d), lambda i: (i, rb)),
                  pl.BlockSpec((1, dv), lambda i: (0, 0))],
        out_specs=pl.BlockSpec((tm, d), lambda i: (i, 0)),
        out_shape=jax.ShapeDtypeStruct((m, d), BF16),
        compiler_params=_params("parallel"),
        name="gla_out_norm",
    )(o, qkvr, g_o.reshape(1, dv))


def _conv_layer(x, nb, nt, hist, g_mix, w_in, w_dw, g_norm, w_out):
    m, d = x.shape
    width = w_dw.shape[0]
    u = _conv_in(x, g_mix, w_in)
    u3 = u.reshape(nb, nt, d)
    if hist is None:
        hist_pad = jnp.zeros((nb, CONV_HALO, d), F32)
        tail = u3[:, nt - (width - 1):]
    else:
        hist_pad = jnp.pad(hist, ((0, 0), (CONV_HALO - (width - 1), 0), (0, 0)))
        tail = jnp.concatenate([hist, u3], axis=1)[:, nt:]
    a = _conv_norm_silu(u3, hist_pad, w_dw, g_norm)
    return _out_proj(a.reshape(m, d), w_out, x), tail


def _pad_cols(w, n):
    return jnp.pad(w, ((0, 0), (0, n - w.shape[1])))


def _fox_layer(xp, xs, dims, cache_k, cache_v, cache_logf, g_mix, w_qkv, w_f, b_f, g_q, g_k, w_o):
    nb, nt, nbs, nts = dims
    d = xp.shape[1]
    heads, hd = cache_k.shape[2], cache_k.shape[3]
    past = cache_k.shape[1]
    hb = FLASH_HEADS_PER_STEP if heads % FLASH_HEADS_PER_STEP == 0 else 1
    w_qkv = w_qkv.astype(BF16)
    w_o = w_o.astype(BF16)
    w_fp = _pad_cols(w_f, LANES).astype(BF16)
    b_fp = jnp.pad(b_f, (0, LANES - heads)).reshape(1, LANES)
    gain_row = jnp.concatenate([jnp.tile(g_q, heads) * (hd ** -0.5 * LOG2E), jnp.tile(g_k, heads),
                                jnp.ones((d,), F32)]).reshape(1, 3 * d)

    def project(x, b_, t_):
        q, k, v, lf = _fox_proj(x, g_mix, w_qkv, gain_row, w_fp, b_fp, hd)
        lf = lf[:, :heads].reshape(b_, t_, heads)
        new = (k.reshape(b_, t_, heads, hd), v.reshape(b_, t_, heads, hd), lf)
        return q.reshape(b_, t_, d), k.reshape(b_, t_, d), v.reshape(b_, t_, d), jnp.transpose(lf, (0, 2, 1)), new

    q, k, v, lf_t, new_p = project(xp, nb, nt)
    c = _cumsum_lanes(lf_t)
    o = _flash_attention(q, k, v, c.reshape(nb, heads // hb, hb, nt), heads, hb)
    yp = _out_proj(o.reshape(nb * nt, d), w_o, xp)

    qs, ks, vs, lfs_t, new_s = project(xs, nbs, nts)
    c_past = _cumsum_lanes(jnp.transpose(cache_logf, (0, 2, 1)), sub_last=True)
    nts_pad = -(-nts // LANES) * LANES
    c_new = _cumsum_lanes(jnp.pad(lfs_t, ((0, 0), (0, 0), (0, nts_pad - nts))))[:, :, :nts]
    os_ = _decode_attention(qs, ks, vs, cache_k.reshape(nbs, past, d), cache_v.reshape(nbs, past, d),
                            c_past.reshape(nbs, heads, 1, past), c_new.reshape(nbs, heads, 1, nts), heads)
    ys = _out_proj(os_.reshape(nbs * nts, d), w_o, xs)
    return yp, ys, new_p + new_s


def _gla_group(x, nb, nt, s0t, chunk, heads, dk, dv, g_mix, w_qkvr, w_a1p, w_a2p, b_a, g_o, w_o):
    m, d = x.shape
    qkvr, la = _gla_proj(x, g_mix, w_qkvr, w_a1p, w_a2p, b_a)
    n = qkvr.shape[1]
    o, st = _gla(qkvr.reshape(nb, nt, n), la.reshape(nb, nt, heads * dk), s0t, heads, dk, dv, chunk)
    a = _gla_out(o.reshape(m, d), qkvr, g_o, heads, dv)
    return _out_proj(a, w_o, x), jnp.swapaxes(st, 2, 3)


def _gla_layer(xp, xs, dims, state, g_mix, w_qkvr, w_a1, w_a2, b_a, g_o, w_o):
    nb, nt, nbs, nts = dims
    heads, dk, dv = state.shape[1], state.shape[2], state.shape[3]
    rank = w_a1.shape[1]
    w_qkvr = w_qkvr.astype(BF16)
    w_o = w_o.astype(BF16)
    w_a1p = _pad_cols(w_a1, LANES).astype(BF16)
    w_a2p = jnp.pad(w_a2, ((0, LANES - rank), (0, 0))).astype(BF16)
    common = (heads, dk, dv, g_mix, w_qkvr, w_a1p, w_a2p, b_a, g_o, w_o)
    zeros = jnp.zeros((nb, heads, dv, dk), F32)
    yp, sp = _gla_group(xp, nb, nt, zeros, GLA_CHUNK, *common)
    ys, ss = _gla_group(xs, nbs, nts, jnp.swapaxes(state, 2, 3), nts, *common)
    return yp, ys, (sp, ss)


def kernel(x_prompt, x_sample, cache_conv_l0, cache_k_l1, cache_v_l1, cache_logf_l1, state_gla_l2, cache_conv_l3, norm_mix_l0, conv_w_in_l0, conv_w_dw_l0, conv_norm_l0, conv_w_out_l0, norm_ffn_l0, ffn_w_up_l0, ffn_w_down_l0, norm_mix_l1, fox_w_qkv_l1, fox_w_f_l1, fox_b_f_l1, fox_q_norm_l1, fox_k_norm_l1, fox_w_o_l1, norm_ffn_l1, ffn_w_up_l1, ffn_w_down_l1, norm_mix_l2, gla_w_qkvr_l2, gla_w_a1_l2, gla_w_a2_l2, gla_b_a_l2, gla_o_norm_l2, gla_w_o_l2, norm_ffn_l2, ffn_w_up_l2, ffn_w_down_l2, norm_mix_l3, conv_w_in_l3, conv_w_dw_l3, conv_norm_l3, conv_w_out_l3, norm_ffn_l3, ffn_w_up_l3, ffn_w_down_l3):
    nb, nt, d = x_prompt.shape
    nbs, nts, _ = x_sample.shape
    dims = (nb, nt, nbs, nts)
    xp = x_prompt.reshape(nb * nt, d)
    xs = x_sample.reshape(nbs * nts, d)

    def ffn(xp, xs, g, wu, wd):
        wu = wu.astype(BF16)
        wd = wd.astype(BF16)
        return _ffn(xp, g, wu, wd), _ffn(xs, g, wu, wd)

    def conv(xp, xs, cache, g_mix, w_in, w_dw, g_norm, w_out):
        w_in = w_in.astype(BF16)
        w_out = w_out.astype(BF16)
        yp, sp = _conv_layer(xp, nb, nt, None, g_mix, w_in, w_dw, g_norm, w_out)
        ys, ss = _conv_layer(xs, nbs, nts, cache, g_mix, w_in, w_dw, g_norm, w_out)
        return yp, ys, (sp, ss)

    xp, xs, st0 = conv(xp, xs, cache_conv_l0, norm_mix_l0, conv_w_in_l0, conv_w_dw_l0, conv_norm_l0, conv_w_out_l0)
    xp, xs = ffn(xp, xs, norm_ffn_l0, ffn_w_up_l0, ffn_w_down_l0)
    xp, xs, st1 = _fox_layer(xp, xs, dims, cache_k_l1, cache_v_l1, cache_logf_l1, norm_mix_l1, fox_w_qkv_l1,
                             fox_w_f_l1, fox_b_f_l1, fox_q_norm_l1, fox_k_norm_l1, fox_w_o_l1)
    xp, xs = ffn(xp, xs, norm_ffn_l1, ffn_w_up_l1, ffn_w_down_l1)
    xp, xs, st2 = _gla_layer(xp, xs, dims, state_gla_l2, norm_mix_l2, gla_w_qkvr_l2, gla_w_a1_l2, gla_w_a2_l2,
                             gla_b_a_l2, gla_o_norm_l2, gla_w_o_l2)
    xp, xs = ffn(xp, xs, norm_ffn_l2, ffn_w_up_l2, ffn_w_down_l2)
    xp, xs, st3 = conv(xp, xs, cache_conv_l3, norm_mix_l3, conv_w_in_l3, conv_w_dw_l3, conv_norm_l3, conv_w_out_l3)
    xp, xs = ffn(xp, xs, norm_ffn_l3, ffn_w_up_l3, ffn_w_down_l3)
    return (xp.reshape(nb, nt, d), xs.reshape(nbs, nts, d), *st0, *st1, *st2, *st3)
```

```python
import functools
import math

import jax
import jax.numpy as jnp
from jax import lax
from jax.experimental import pallas as pl
from jax.experimental.pallas import tpu as pltpu

F32 = jnp.float32
BF16 = jnp.bfloat16

EPS = 1e-6
LOG2E = math.log2(math.e)
GLA_CHUNK = 64
GLA_SUB = 8
GLA_CHUNKS_PER_STEP = 2
GLA_TAU = 16.0
FLASH_ROW_CHUNK = 32
FLASH_HEADS_PER_STEP = 4
LANES = 128
SUBLANES = 8
BF16_ROWS = 2 * SUBLANES
CONV_HALO = 32
VMEM_LIMIT_BYTES = 56 * 1024 * 1024


def _tile(n, pref):
    if n <= pref:
        return n
    t = pref - pref % 8
    while t >= 8:
        if n % t == 0:
            return t
        t -= 8
    return n


def _params(*sem):
    return pltpu.CompilerParams(dimension_semantics=sem, vmem_limit_bytes=VMEM_LIMIT_BYTES)


def _log_sigmoid(x):
    return jnp.minimum(x, 0.0) - jnp.log(1.0 + jnp.exp(-jnp.abs(x)))


def _rms_bf16(x, g):
    ms = jnp.mean(x * x, axis=-1, keepdims=True)
    return (x * lax.rsqrt(ms + EPS) * g).astype(BF16)


def _dot(a, b):
    return jnp.dot(a, b, preferred_element_type=F32)


def _dot_nt(a, b):
    return lax.dot_general(a, b, (((1,), (1,)), ((), ())), preferred_element_type=F32)


def _dot_tn(a, b):
    return lax.dot_general(a, b, (((0,), (0,)), ((), ())), preferred_element_type=F32)


def _conv_in_kernel(x_ref, g_ref, wa_ref, wb_ref, o_ref, h_scr):
    @pl.when(pl.program_id(1) == 0)
    def _():
        h_scr[...] = _rms_bf16(x_ref[...], g_ref[...])

    h = h_scr[...]
    o_ref[...] = _dot(h, wa_ref[...]) * jax.nn.sigmoid(_dot(h, wb_ref[...]))


def _conv_in(x, g, w_in):
    m, d = x.shape
    n = w_in.shape[1] // 2
    tm = _tile(m, 1024)
    tn = _tile(n, 512)
    nb = n // tn
    return pl.pallas_call(
        _conv_in_kernel,
        grid=(m // tm, nb),
        in_specs=[pl.BlockSpec((tm, d), lambda i, j: (i, 0)),
                  pl.BlockSpec((1, d), lambda i, j: (0, 0)),
                  pl.BlockSpec((d, tn), lambda i, j: (0, j)),
                  pl.BlockSpec((d, tn), lambda i, j: (0, j + nb))],
        out_specs=pl.BlockSpec((tm, tn), lambda i, j: (i, j)),
        out_shape=jax.ShapeDtypeStruct((m, n), F32),
        scratch_shapes=[pltpu.VMEM((tm, d), BF16)],
        compiler_params=_params("parallel", "arbitrary"),
        name="conv_in_glu",
    )(x, g.reshape(1, d), w_in, w_in)


def _fox_proj_kernel(x_ref, g_ref, w_ref, gain_ref, wf_ref, bf_ref, q_ref, k_ref, v_ref, k4_ref, v4_ref, lf_ref,
                     h_scr, *, nq, heads_per_tile, head_dim):
    j = pl.program_id(1)

    @pl.when(j == 0)
    def _():
        h = _rms_bf16(x_ref[...], g_ref[...])
        h_scr[...] = h
        lf_ref[...] = _log_sigmoid(_dot(h, wf_ref[...]) + bf_ref[...])

    acc = _dot(h_scr[...], w_ref[...])

    def head_norm(h):
        sl = slice(h * head_dim, (h + 1) * head_dim)
        x = acc[:, sl]
        ms = jnp.mean(x * x, axis=-1, keepdims=True)
        return sl, x * lax.rsqrt(ms + EPS) * gain_ref[:, sl]

    @pl.when(j < nq)
    def _():
        for h in range(heads_per_tile):
            sl, y = head_norm(h)
            q_ref[:, sl] = y.astype(q_ref.dtype)

    @pl.when((j >= nq) & (j < 2 * nq))
    def _():
        for h in range(heads_per_tile):
            sl, y = head_norm(h)
            k_ref[:, sl] = y.astype(k_ref.dtype)
            k4_ref[:, h, :] = y

    @pl.when(j >= 2 * nq)
    def _():
        v_ref[...] = acc.astype(v_ref.dtype)
        for h in range(heads_per_tile):
            v4_ref[:, h, :] = acc[:, h * head_dim:(h + 1) * head_dim]


def _fox_proj(x, g, w_qkv, gain_row, w_fp, b_fp, heads):
    m, d = x.shape
    n = w_qkv.shape[1]
    head_dim = d // heads
    hpt = min(heads, SUBLANES)
    tn = hpt * head_dim
    tm = _tile(m, 512)
    nq = d // tn
    assert d % tn == 0 and n == 3 * d

    def cols(first):
        return lambda i, j: (i, jnp.clip(j - first, 0, nq - 1))

    def cols3(first):
        return lambda i, j: (i, jnp.clip(j - first, 0, nq - 1), 0)

    return pl.pallas_call(
        functools.partial(_fox_proj_kernel, nq=nq, heads_per_tile=hpt, head_dim=head_dim),
        grid=(m // tm, n // tn),
        in_specs=[pl.BlockSpec((tm, d), lambda i, j: (i, 0)),
                  pl.BlockSpec((1, d), lambda i, j: (0, 0)),
                  pl.BlockSpec((d, tn), lambda i, j: (0, j)),
                  pl.BlockSpec((1, tn), lambda i, j: (0, j)),
                  pl.BlockSpec((d, LANES), lambda i, j: (0, 0)),
                  pl.BlockSpec((1, LANES), lambda i, j: (0, 0))],
        out_specs=[pl.BlockSpec((tm, tn), cols(0)),
                   pl.BlockSpec((tm, tn), cols(nq)),
                   pl.BlockSpec((tm, tn), cols(2 * nq)),
                   pl.BlockSpec((tm, hpt, head_dim), cols3(nq)),
                   pl.BlockSpec((tm, hpt, head_dim), cols3(2 * nq)),
                   pl.BlockSpec((tm, LANES), lambda i, j: (i, 0))],
        out_shape=[jax.ShapeDtypeStruct((m, d), BF16), jax.ShapeDtypeStruct((m, d), BF16),
                   jax.ShapeDtypeStruct((m, d), BF16), jax.ShapeDtypeStruct((m, heads, head_dim), F32),
                   jax.ShapeDtypeStruct((m, heads, head_dim), F32), jax.ShapeDtypeStruct((m, LANES), F32)],
        scratch_shapes=[pltpu.VMEM((tm, d), BF16)],
        compiler_params=_params("parallel", "arbitrary"),
        name="fox_proj",
    )(x, g.reshape(1, d), w_qkv, gain_row, w_fp, b_fp)


def _gla_proj_kernel(x_ref, g_ref, w_ref, w1_ref, w2_ref, ba_ref, o_ref, la_ref, h_scr):
    @pl.when(pl.program_id(1) == 0)
    def _():
        h = _rms_bf16(x_ref[...], g_ref[...])
        h_scr[...] = h
        low = _dot(h, w1_ref[...]).astype(BF16)
        la_ref[...] = _log_sigmoid(_dot(low, w2_ref[...]) + ba_ref[...]) * (1.0 / GLA_TAU)

    o_ref[...] = _dot(h_scr[...], w_ref[...])


def _gla_proj(x, g, w_qkvr, w_a1p, w_a2p, b_a):
    m, d = x.shape
    n = w_qkvr.shape[1]
    na = w_a2p.shape[1]
    tm = _tile(m, 1024)
    tn = _tile(n, 1024)
    return pl.pallas_call(
        _gla_proj_kernel,
        grid=(m // tm, n // tn),
        in_specs=[pl.BlockSpec((tm, d), lambda i, j: (i, 0)),
                  pl.BlockSpec((1, d), lambda i, j: (0, 0)),
                  pl.BlockSpec((d, tn), lambda i, j: (0, j)),
                  pl.BlockSpec(w_a1p.shape, lambda i, j: (0, 0)),
                  pl.BlockSpec(w_a2p.shape, lambda i, j: (0, 0)),
                  pl.BlockSpec((1, na), lambda i, j: (0, 0))],
        out_specs=[pl.BlockSpec((tm, tn), lambda i, j: (i, j)),
                   pl.BlockSpec((tm, na), lambda i, j: (i, 0))],
        out_shape=[jax.ShapeDtypeStruct((m, n), F32), jax.ShapeDtypeStruct((m, na), F32)],
        scratch_shapes=[pltpu.VMEM((tm, d), BF16)],
        compiler_params=_params("parallel", "arbitrary"),
        name="gla_proj",
    )(x, g.reshape(1, d), w_qkvr, w_a1p, w_a2p, b_a.reshape(1, na))


def _out_proj_kernel(a_ref, w_ref, r_ref, o_ref):
    o_ref[...] = r_ref[...] + _dot(a_ref[...], w_ref[...])


def _out_proj(a, w, res):
    m, k = a.shape
    n = w.shape[1]
    tm = _tile(m, 1024)
    tn = _tile(n, 1024)
    return pl.pallas_call(
        _out_proj_kernel,
        grid=(m // tm, n // tn),
        in_specs=[pl.BlockSpec((tm, k), lambda i, j: (i, 0)),
                  pl.BlockSpec((k, tn), lambda i, j: (0, j)),
                  pl.BlockSpec((tm, tn), lambda i, j: (i, j))],
        out_specs=pl.BlockSpec((tm, tn), lambda i, j: (i, j)),
        out_shape=jax.ShapeDtypeStruct((m, n), F32),
        compiler_params=_params("parallel", "arbitrary"),
        name="out_proj",
    )(a, w, res)


def _ffn_kernel(x_ref, g_ref, wu_ref, wd_ref, o_ref, h_scr):
    @pl.when(pl.program_id(1) == 0)
    def _():
        x = x_ref[...]
        h_scr[...] = _rms_bf16(x, g_ref[...])
        o_ref[...] = x

    u = jnp.maximum(_dot(h_scr[...], wu_ref[...]), 0.0)
    u = (u * u).astype(BF16)
    d = o_ref.shape[1]
    dc = min(d, 512)
    for c in range(d // dc):
        cs = slice(c * dc, (c + 1) * dc)
        o_ref[:, cs] += _dot(u, wd_ref[:, cs])


def _ffn(x, g, wu, wd):
    m, d = x.shape
    f = wu.shape[1]
    tm = _tile(m, 1024)
    tf = _tile(f, 512)
    return pl.pallas_call(
        _ffn_kernel,
        grid=(m // tm, f // tf),
        in_specs=[pl.BlockSpec((tm, d), lambda i, j: (i, 0)),
                  pl.BlockSpec((1, d), lambda i, j: (0, 0)),
                  pl.BlockSpec((d, tf), lambda i, j: (0, j)),
                  pl.BlockSpec((tf, d), lambda i, j: (j, 0))],
        out_specs=pl.BlockSpec((tm, d), lambda i, j: (i, 0)),
        out_shape=jax.ShapeDtypeStruct((m, d), F32),
        scratch_shapes=[pltpu.VMEM((tm, d), BF16)],
        compiler_params=_params("parallel", "arbitrary"),
        name="ffn",
    )(x, g.reshape(1, d), wu, wd)


def _shift_rows_up(x, r):
    n, c = x.shape
    x3 = x.reshape(n // SUBLANES, SUBLANES, c)
    rot = pltpu.roll(x3, SUBLANES - r, axis=1)
    sub = lax.broadcasted_iota(jnp.int32, (n // SUBLANES - 1, SUBLANES, c), 1)
    return jnp.where(sub < SUBLANES - r, rot[:-1], rot[1:]).reshape(n - SUBLANES, c)


def _conv_kernel(u_ref, up_ref, hist_ref, w_ref, g_ref, o_ref, ext_scr, y_scr, *, tm, width, rb, lc):
    i = pl.program_id(1)
    d = u_ref.shape[-1]
    nr = BF16_ROWS

    @pl.when(i == 0)
    def _():
        ext_scr[0:CONV_HALO, :] = hist_ref[...]

    @pl.when(i > 0)
    def _():
        ext_scr[0:CONV_HALO, :] = up_ref[...]

    ext_scr[CONV_HALO:CONV_HALO + tm, :] = u_ref[...]
    off = CONV_HALO - (width - 1)

    def row_block(r, carry):
        r0 = pl.multiple_of(r * rb, rb)
        for c in range(d // lc):
            cs = slice(c * lc, (c + 1) * lc)
            acc = jnp.zeros((rb, lc), F32)
            for rem in range(SUBLANES):
                part = None
                for k in range(width):
                    if (off + k) % SUBLANES != rem:
                        continue
                    base = (off + k) // SUBLANES * SUBLANES
                    rows = rb + SUBLANES if rem else rb
                    win = ext_scr[pl.ds(r0 + base, rows), cs].reshape(rows // SUBLANES, SUBLANES, lc)
                    term = (win * w_ref[k * SUBLANES:(k + 1) * SUBLANES, cs][None]).reshape(rows, lc)
                    part = term if part is None else part + term
                if part is None:
                    continue
                acc = acc + (_shift_rows_up(part, rem) if rem else part)
            y_scr[pl.ds(r0, rb), cs] = acc
        for s in range(rb // nr):
            rows = pl.ds(r0 + s * nr, nr)
            y = y_scr[rows, :]
            ms = jnp.mean(y * y, axis=-1, keepdims=True)
            z = y * lax.rsqrt(ms + EPS) * g_ref[...]
            o_ref[rows, :] = (z * jax.nn.sigmoid(z)).astype(o_ref.dtype)
        return carry

    lax.fori_loop(0, tm // rb, row_block, 0)


def _conv_norm_silu(u, hist_pad, w_dw, g):
    nb, nt, d = u.shape
    width = w_dw.shape[0]
    assert width - 1 <= CONV_HALO
    tm = _tile(nt, 512)
    assert tm % CONV_HALO == 0
    rb = 64 if tm % 64 == 0 else CONV_HALO
    lc = min(256, d)
    hb = tm // CONV_HALO
    w_rep = jnp.repeat(w_dw, SUBLANES, axis=0)
    return pl.pallas_call(
        functools.partial(_conv_kernel, tm=tm, width=width, rb=rb, lc=lc),
        grid=(nb, nt // tm),
        in_specs=[pl.BlockSpec((None, tm, d), lambda b, i: (b, i, 0)),
                  pl.BlockSpec((None, CONV_HALO, d), lambda b, i: (b, jnp.maximum(i * hb - 1, 0), 0)),
                  pl.BlockSpec((None, CONV_HALO, d), lambda b, i: (b, 0, 0)),
                  pl.BlockSpec((width * SUBLANES, d), lambda b, i: (0, 0)),
                  pl.BlockSpec((1, d), lambda b, i: (0, 0))],
        out_specs=pl.BlockSpec((None, tm, d), lambda b, i: (b, i, 0)),
        out_shape=jax.ShapeDtypeStruct((nb, nt, d), BF16),
        scratch_shapes=[pltpu.VMEM((CONV_HALO + tm, d), F32), pltpu.VMEM((tm, d), F32)],
        compiler_params=_params("parallel", "arbitrary"),
        name="conv_norm_silu",
    )(u, u, hist_pad, w_rep, g.reshape(1, d))


def _cumsum_kernel(x_ref, o_ref, *, n, sub_last, scale):
    x = x_ref[...]
    lane = lax.broadcasted_iota(jnp.int32, x.shape, 1)
    s = 1
    while s < n:
        x = x + jnp.where(lane >= s, pltpu.roll(x, s, axis=1), 0.0)
        s *= 2
    if sub_last:
        x = x - x[:, n - 1:n]
    o_ref[...] = x * scale


def _cumsum_lanes(x, *, sub_last=False, scale=LOG2E):
    nb, h, n = x.shape
    return pl.pallas_call(
        functools.partial(_cumsum_kernel, n=n, sub_last=sub_last, scale=scale),
        grid=(nb,),
        in_specs=[pl.BlockSpec((None, h, n), lambda b: (b, 0, 0))],
        out_specs=pl.BlockSpec((None, h, n), lambda b: (b, 0, 0)),
        out_shape=jax.ShapeDtypeStruct((nb, h, n), F32),
        compiler_params=_params("parallel"),
        name="cumsum_logf",
    )(x)


def _flash_kernel(q_ref, k_ref, v_ref, c_ref, o_ref, s_scr, p_scr, m_scr, l_scr, acc_scr, *, tq, hd, hb, rc):
    qi = pl.program_id(2)
    heads = [slice(h * hd, (h + 1) * hd) for h in range(hb)]
    qs = [q_ref[:, hs] for hs in heads]
    m_scr[...] = jnp.full(m_scr.shape, -jnp.inf, F32)
    l_scr[...] = jnp.zeros(l_scr.shape, F32)
    acc_scr[...] = jnp.zeros(acc_scr.shape, F32)

    def step(kj, masked):
        k0 = pl.multiple_of(kj * tq, tq)
        for h, hs in enumerate(heads):
            s_scr[h] = _dot_nt(qs[h], k_ref[pl.ds(k0, tq), hs])
            c_row = c_ref[h:h + 1, pl.ds(k0, tq)]
            m_all = m_scr[h]
            l_all = l_scr[h]
            m_out, l_out, a_out = [], [], []
            for r in range(tq // rc):
                rows = slice(r * rc, (r + 1) * rc)
                nvis = min(tq, (r + 1) * rc + LANES - 1) // LANES if masked else tq // LANES
                blocks = []
                for i in range(nvis):
                    cols = slice(i * LANES, (i + 1) * LANES)
                    blk = s_scr[h, rows, cols] - c_row[:, cols]
                    if masked and (i + 1) * LANES - 1 > r * rc:
                        row = lax.broadcasted_iota(jnp.int32, (rc, LANES), 0) + r * rc
                        col = lax.broadcasted_iota(jnp.int32, (rc, LANES), 1) + i * LANES
                        blk = jnp.where(col <= row, blk, -jnp.inf)
                    blocks.append(blk)
                m_old = m_all[rows]
                m_new = jnp.maximum(m_old, jnp.max(functools.reduce(jnp.maximum, blocks), axis=-1, keepdims=True))
                alpha = jnp.exp2(m_old - m_new)
                ps = [jnp.exp2(blk - m_new) for blk in blocks]
                l_out.append(alpha * l_all[rows] + jnp.sum(functools.reduce(jnp.add, ps), axis=-1, keepdims=True))
                m_out.append(m_new)
                a_out.append(alpha)
                for i in range(tq // LANES):
                    pb = ps[i].astype(BF16) if i < nvis else jnp.zeros((rc, LANES), BF16)
                    p_scr[h, rows, i * LANES:(i + 1) * LANES] = pb
            m_scr[h] = jnp.concatenate(m_out, axis=0)
            l_scr[h] = jnp.concatenate(l_out, axis=0)
            alpha = jnp.concatenate(a_out, axis=0)
            acc_scr[h] = alpha * acc_scr[h] + _dot(p_scr[h], v_ref[pl.ds(k0, tq), hs])

    def body(kj, carry):
        step(kj, False)
        return carry

    lax.fori_loop(0, qi, body, 0)
    step(qi, True)
    for h, hs in enumerate(heads):
        o_ref[:, hs] = (acc_scr[h] / l_scr[h]).astype(o_ref.dtype)


def _flash_attention(q, k, v, c, heads, hb):
    nb, nt, d = q.shape
    hd = d // heads
    assert hd == LANES
    tq = _tile(nt, 512)
    w = hb * hd
    nhb = heads // hb
    return pl.pallas_call(
        functools.partial(_flash_kernel, tq=tq, hd=hd, hb=hb, rc=min(FLASH_ROW_CHUNK, tq)),
        grid=(nb, nhb, nt // tq),
        in_specs=[pl.BlockSpec((None, tq, w), lambda b, h, i: (b, i, h)),
                  pl.BlockSpec((None, nt, w), lambda b, h, i: (b, 0, h)),
                  pl.BlockSpec((None, nt, w), lambda b, h, i: (b, 0, h)),
                  pl.BlockSpec((None, None, hb, nt), lambda b, h, i: (b, h, 0, 0))],
        out_specs=pl.BlockSpec((None, tq, w), lambda b, h, i: (b, i, h)),
        out_shape=jax.ShapeDtypeStruct((nb, nt, d), BF16),
        scratch_shapes=[pltpu.VMEM((hb, tq, tq), F32), pltpu.VMEM((hb, tq, tq), BF16),
                        pltpu.VMEM((hb, tq, LANES), F32), pltpu.VMEM((hb, tq, LANES), F32),
                        pltpu.VMEM((hb, tq, hd), F32)],
        compiler_params=_params("parallel", "parallel", "arbitrary"),
        name="fox_flash",
    )(q, k, v, c)


def _decode_kernel(q_ref, kn_ref, vn_ref, kc_ref, vc_ref, cp_ref, cn_ref, o_ref, *, hd, hb):
    ts = q_ref.shape[0]
    row = lax.broadcasted_iota(jnp.int32, (ts, ts), 0)
    col = lax.broadcasted_iota(jnp.int32, (ts, ts), 1)
    for h in range(hb):
        hs = slice(h * hd, (h + 1) * hd)
        q = q_ref[:, hs]
        s_p = _dot_nt(q, kc_ref[:, hs].astype(BF16)) - cp_ref[h:h + 1, :]
        s_n = _dot_nt(q, kn_ref[:, hs]) - cn_ref[h:h + 1, :]
        s_n = jnp.where(col <= row, s_n, -jnp.inf)
        m = jnp.maximum(jnp.max(s_p, axis=-1, keepdims=True), jnp.max(s_n, axis=-1, keepdims=True))
        p_p = jnp.exp2(s_p - m)
        p_n = jnp.exp2(s_n - m)
        l = jnp.sum(p_p, axis=-1, keepdims=True) + jnp.sum(p_n, axis=-1, keepdims=True)
        acc = (_dot(p_p.astype(BF16), vc_ref[:, hs].astype(BF16))
               + _dot(p_n.astype(BF16), vn_ref[:, hs]))
        o_ref[:, hs] = (acc / l).astype(o_ref.dtype)


def _decode_attention(q, k_new, v_new, k_cache, v_cache, c_past, c_new, heads, hb):
    nb, ts, d = q.shape
    p = k_cache.shape[1]
    hd = d // heads
    w = hb * hd
    return pl.pallas_call(
        functools.partial(_decode_kernel, hd=hd, hb=hb),
        grid=(nb, heads // hb),
        in_specs=[pl.BlockSpec((None, ts, w), lambda b, h: (b, 0, h)),
                  pl.BlockSpec((None, ts, w), lambda b, h: (b, 0, h)),
                  pl.BlockSpec((None, ts, w), lambda b, h: (b, 0, h)),
                  pl.BlockSpec((None, p, w), lambda b, h: (b, 0, h)),
                  pl.BlockSpec((None, p, w), lambda b, h: (b, 0, h)),
                  pl.BlockSpec((None, None, hb, p), lambda b, h: (b, h, 0, 0)),
                  pl.BlockSpec((None, None, hb, ts), lambda b, h: (b, h, 0, 0))],
        out_specs=pl.BlockSpec((None, ts, w), lambda b, h: (b, 0, h)),
        out_shape=jax.ShapeDtypeStruct((nb, ts, d), BF16),
        compiler_params=_params("parallel", "parallel"),
        name="fox_decode",
    )(q, k_new, v_new, k_cache, v_cache, c_past, c_new)


def _gla_chunk(q, k, v, la2, st, chunk, sub):
    row = lax.broadcasted_iota(jnp.int32, q.shape, 0)
    b = la2
    s = 1
    while s < chunk:
        b = b + jnp.where(row >= s, pltpu.roll(b, s, axis=0), 0.0)
        s *= 2
    b_last = b[chunk - 1:chunk, :]

    o = _dot_nt((q * jnp.exp2(b)).astype(BF16), st.astype(BF16))

    lane = lax.broadcasted_iota(jnp.int32, (sub, chunk), 1)
    srow = lax.broadcasted_iota(jnp.int32, (sub, chunk), 0)
    blocks = []
    for i in range(chunk // sub):
        r0 = i * sub
        qi = q[r0:r0 + sub]
        bi = b[r0:r0 + sub]
        diag = jnp.zeros((sub, chunk), F32)
        for j in range(sub):
            kj = k[r0 + j:r0 + j + 1]
            bj = b[r0 + j:r0 + j + 1]
            col = jnp.sum(qi * kj * jnp.exp2(bi - bj), axis=-1, keepdims=True)
            diag = jnp.where(lane == r0 + j, col, diag)
        a_i = jnp.where(lane - r0 <= srow, diag, 0.0)
        if i > 0:
            ref_b = b[r0:r0 + 1]
            qg = (qi * jnp.exp2(bi - ref_b)).astype(BF16)
            rr = min(chunk, -(-r0 // BF16_ROWS) * BF16_ROWS)
            kg = (k[:rr] * jnp.exp2(jnp.minimum(ref_b - b[:rr], 0.0))).astype(BF16)
            if rr < chunk:
                kg = jnp.concatenate([kg, jnp.zeros((chunk - rr, kg.shape[1]), BF16)], axis=0)
            a_i = jnp.where(lane < r0, _dot_nt(qg, kg), a_i)
        blocks.append(a_i)
    a = jnp.concatenate(blocks, axis=0)
    o = o + _dot(a.astype(BF16), v)

    kd = (k * jnp.exp2(b_last - b)).astype(BF16)
    st_new = st * jnp.exp2(b_last) + _dot_tn(v, kd)
    return o, st_new


def _gla_kernel(q_ref, k_ref, v_ref, la_ref, s0_ref, o_ref, s_out_ref, s_scr, *, chunk, sub, cps, nsteps, scale):
    step = pl.program_id(2)

    @pl.when(step == 0)
    def _():
        s_scr[...] = s0_ref[...]

    st = s_scr[...]
    for ci in range(cps):
        rows = slice(ci * chunk, (ci + 1) * chunk)
        o, st = _gla_chunk(q_ref[rows, :] * scale, k_ref[rows, :], v_ref[rows, :].astype(BF16),
                           la_ref[rows, :] * LOG2E, st, chunk, sub)
        o_ref[rows, :] = o
    s_scr[...] = st

    @pl.when(step == nsteps - 1)
    def _():
        s_out_ref[...] = st


def _gla(qkvr, la, s0t, heads, dk, dv, chunk):
    nb, nt, _ = qkvr.shape
    cps = GLA_CHUNKS_PER_STEP if nt % (GLA_CHUNKS_PER_STEP * chunk) == 0 else 1
    rows = cps * chunk
    nsteps = nt // rows
    kb = heads
    vb = 2 * heads * dk // dv
    assert (2 * heads * dk) % dv == 0
    return pl.pallas_call(
        functools.partial(_gla_kernel, chunk=chunk, sub=min(GLA_SUB, chunk), cps=cps, nsteps=nsteps, scale=dk ** -0.5),
        grid=(nb, heads, nsteps),
        in_specs=[pl.BlockSpec((None, rows, dk), lambda b, h, c: (b, c, h)),
                  pl.BlockSpec((None, rows, dk), lambda b, h, c: (b, c, kb + h)),
                  pl.BlockSpec((None, rows, dv), lambda b, h, c: (b, c, vb + h)),
                  pl.BlockSpec((None, rows, dk), lambda b, h, c: (b, c, h)),
                  pl.BlockSpec((None, None, dv, dk), lambda b, h, c: (b, h, 0, 0))],
        out_specs=[pl.BlockSpec((None, rows, dv), lambda b, h, c: (b, c, h)),
                   pl.BlockSpec((None, None, dv, dk), lambda b, h, c: (b, h, 0, 0))],
        out_shape=[jax.ShapeDtypeStruct((nb, nt, heads * dv), F32),
                   jax.ShapeDtypeStruct((nb, heads, dv, dk), F32)],
        scratch_shapes=[pltpu.VMEM((dv, dk), F32)],
        compiler_params=_params("parallel", "parallel", "arbitrary"),
        name="gla_chunk",
    )(qkvr, qkvr, qkvr, la, s0t)


def _gla_out_kernel(o_ref, r_ref, g_ref, a_ref, *, heads, dv):
    g = g_ref[...]
    for h in range(heads):
        sl = slice(h * dv, (h + 1) * dv)
        x = o_ref[:, sl]
        r = r_ref[:, sl]
        ms = jnp.mean(x * x, axis=-1, keepdims=True)
        a_ref[:, sl] = (x * lax.rsqrt(ms + EPS) * g * (r * jax.nn.sigmoid(r))).astype(a_ref.dtype)


def _gla_out(o, qkvr, g_o, heads, dv):
    m, d = o.shape
    tm = _tile(m, 512)
    rb = qkvr.shape[1] // d - 1
    return pl.pallas_call(
        functools.partial(_gla_out_kernel, heads=heads, dv=dv),
        grid=(m // tm,),
        in_specs=[pl.BlockSpec((tm, d), lambda i: (i, 0)),
                  pl.BlockSpec((tm, d), lambda i: (i, rb)),
                  pl.BlockSpec((1, dv), lambda i: (0, 0))],
        out_specs=pl.BlockSpec((tm, d), lambda i: (i, 0)),
        out_shape=jax.ShapeDtypeStruct((m, d), BF16),
        compiler_params=_params("parallel"),
        name="gla_out_norm",
    )(o, qkvr, g_o.reshape(1, dv))


def _conv_layer(x, nb, nt, hist, g_mix, w_in, w_dw, g_norm, w_out):
    m, d = x.shape
    width = w_dw.shape[0]
    u = _conv_in(x, g_mix, w_in)
    u3 = u.reshape(nb, nt, d)
    if hist is None:
        hist_pad = jnp.zeros((nb, CONV_HALO, d), F32)
        tail = u3[:, nt - (width - 1):]
    else:
        hist_pad = jnp.pad(hist, ((0, 0), (CONV_HALO - (width - 1), 0), (0, 0)))
        tail = jnp.concatenate([hist, u3], axis=1)[:, nt:]
    a = _conv_norm_silu(u3, hist_pad, w_dw, g_norm)
    return _out_proj(a.reshape(m, d), w_out, x), tail


def _pad_cols(w, n):
    return jnp.pad(w, ((0, 0), (0, n - w.shape[1])))


def _fox_layer(xp, xs, dims, cache_k, cache_v, cache_logf, g_mix, w_qkv, w_f, b_f, g_q, g_k, w_o):
    nb, nt, nbs, nts = dims
    d = xp.shape[1]
    heads, hd = cache_k.shape[2], cache_k.shape[3]
    past = cache_k.shape[1]
    hb = FLASH_HEADS_PER_STEP if heads % FLASH_HEADS_PER_STEP == 0 else 1
    w_qkv = w_qkv.astype(BF16)
    w_o = w_o.astype(BF16)
    w_fp = _pad_cols(w_f, LANES).astype(BF16)
    b_fp = jnp.pad(b_f, (0, LANES - heads)).reshape(1, LANES)
    gain_row = jnp.concatenate([jnp.tile(g_q, heads) * (hd ** -0.5 * LOG2E), jnp.tile(g_k, heads),
                                jnp.ones((d,), F32)]).reshape(1, 3 * d)

    def project(x, b_, t_):
        q, k, v, k4, v4, lf = _fox_proj(x, g_mix, w_qkv, gain_row, w_fp, b_fp, heads)
        lf = lf[:, :heads].reshape(b_, t_, heads)
        new = (k4.reshape(b_, t_, heads, hd), v4.reshape(b_, t_, heads, hd), lf)
        return q.reshape(b_, t_, d), k.reshape(b_, t_, d), v.reshape(b_, t_, d), jnp.transpose(lf, (0, 2, 1)), new

    q, k, v, lf_t, new_p = project(xp, nb, nt)
    c = _cumsum_lanes(lf_t)
    o = _flash_attention(q, k, v, c.reshape(nb, heads // hb, hb, nt), heads, hb)
    yp = _out_proj(o.reshape(nb * nt, d), w_o, xp)

    qs, ks, vs, lfs_t, new_s = project(xs, nbs, nts)
    c_past = _cumsum_lanes(jnp.transpose(cache_logf, (0, 2, 1)), sub_last=True)
    nts_pad = -(-nts // LANES) * LANES
    c_new = _cumsum_lanes(jnp.pad(lfs_t, ((0, 0), (0, 0), (0, nts_pad - nts))))[:, :, :nts]
    os_ = _decode_attention(qs, ks, vs, cache_k.reshape(nbs, past, d), cache_v.reshape(nbs, past, d),
                            c_past.reshape(nbs, heads // hb, hb, past), c_new.reshape(nbs, heads // hb, hb, nts),
                            heads, hb)
    ys = _out_proj(os_.reshape(nbs * nts, d), w_o, xs)
    return yp, ys, new_p + new_s


def _gla_group(x, nb, nt, s0t, chunk, heads, dk, dv, g_mix, w_qkvr, w_a1p, w_a2p, b_a, g_o, w_o):
    m, d = x.shape
    qkvr, la = _gla_proj(x, g_mix, w_qkvr, w_a1p, w_a2p, b_a)
    n = qkvr.shape[1]
    o, st = _gla(qkvr.reshape(nb, nt, n), la.reshape(nb, nt, heads * dk), s0t, heads, dk, dv, chunk)
    a = _gla_out(o.reshape(m, d), qkvr, g_o, heads, dv)
    return _out_proj(a, w_o, x), jnp.swapaxes(st, 2, 3)


def _gla_layer(xp, xs, dims, state, g_mix, w_qkvr, w_a1, w_a2, b_a, g_o, w_o):
    nb, nt, nbs, nts = dims
    heads, dk, dv = state.shape[1], state.shape[2], state.shape[3]
    rank = w_a1.shape[1]
    w_qkvr = w_qkvr.astype(BF16)
    w_o = w_o.astype(BF16)
    w_a1p = _pad_cols(w_a1, LANES).astype(BF16)
    w_a2p = jnp.pad(w_a2, ((0, LANES - rank), (0, 0))).astype(BF16)
    common = (heads, dk, dv, g_mix, w_qkvr, w_a1p, w_a2p, b_a, g_o, w_o)
    zeros = jnp.zeros((nb, heads, dv, dk), F32)
    yp, sp = _gla_group(xp, nb, nt, zeros, GLA_CHUNK, *common)
    ys, ss = _gla_group(xs, nbs, nts, jnp.swapaxes(state, 2, 3), nts, *common)
    return yp, ys, (sp, ss)


def kernel(x_prompt, x_sample, cache_conv_l0, cache_k_l1, cache_v_l1, cache_logf_l1, state_gla_l2, cache_conv_l3, norm_mix_l0, conv_w_in_l0, conv_w_dw_l0, conv_norm_l0, conv_w_out_l0, norm_ffn_l0, ffn_w_up_l0, ffn_w_down_l0, norm_mix_l1, fox_w_qkv_l1, fox_w_f_l1, fox_b_f_l1, fox_q_norm_l1, fox_k_norm_l1, fox_w_o_l1, norm_ffn_l1, ffn_w_up_l1, ffn_w_down_l1, norm_mix_l2, gla_w_qkvr_l2, gla_w_a1_l2, gla_w_a2_l2, gla_b_a_l2, gla_o_norm_l2, gla_w_o_l2, norm_ffn_l2, ffn_w_up_l2, ffn_w_down_l2, norm_mix_l3, conv_w_in_l3, conv_w_dw_l3, conv_norm_l3, conv_w_out_l3, norm_ffn_l3, ffn_w_up_l3, ffn_w_down_l3):
    nb, nt, d = x_prompt.shape
    nbs, nts, _ = x_sample.shape
    dims = (nb, nt, nbs, nts)
    xp = x_prompt.reshape(nb * nt, d)
    xs = x_sample.reshape(nbs * nts, d)

    def ffn(xp, xs, g, wu, wd):
        wu = wu.astype(BF16)
        wd = wd.astype(BF16)
        return _ffn(xp, g, wu, wd), _ffn(xs, g, wu, wd)

    def conv(xp, xs, cache, g_mix, w_in, w_dw, g_norm, w_out):
        w_in = w_in.astype(BF16)
        w_out = w_out.astype(BF16)
        yp, sp = _conv_layer(xp, nb, nt, None, g_mix, w_in, w_dw, g_norm, w_out)
        ys, ss = _conv_layer(xs, nbs, nts, cache, g_mix, w_in, w_dw, g_norm, w_out)
        return yp, ys, (sp, ss)

    xp, xs, st0 = conv(xp, xs, cache_conv_l0, norm_mix_l0, conv_w_in_l0, conv_w_dw_l0, conv_norm_l0, conv_w_out_l0)
    xp, xs = ffn(xp, xs, norm_ffn_l0, ffn_w_up_l0, ffn_w_down_l0)
    xp, xs, st1 = _fox_layer(xp, xs, dims, cache_k_l1, cache_v_l1, cache_logf_l1, norm_mix_l1, fox_w_qkv_l1,
                             fox_w_f_l1, fox_b_f_l1, fox_q_norm_l1, fox_k_norm_l1, fox_w_o_l1)
    xp, xs = ffn(xp, xs, norm_ffn_l1, ffn_w_up_l1, ffn_w_down_l1)
    xp, xs, st2 = _gla_layer(xp, xs, dims, state_gla_l2, norm_mix_l2, gla_w_qkvr_l2, gla_w_a1_l2, gla_w_a2_l2,
                             gla_b_a_l2, gla_o_norm_l2, gla_w_o_l2)
    xp, xs = ffn(xp, xs, norm_ffn_l2, ffn_w_up_l2, ffn_w_down_l2)
    xp, xs, st3 = conv(xp, xs, cache_conv_l3, norm_mix_l3, conv_w_in_l3, conv_w_dw_l3, conv_norm_l3, conv_w_out_l3)
    xp, xs = ffn(xp, xs, norm_ffn_l3, ffn_w_up_l3, ffn_w_down_l3)
    return (xp.reshape(nb, nt, d), xs.reshape(nbs, nts, d), *st0, *st1, *st2, *st3)
```

```python
import functools
import math

import jax
import jax.numpy as jnp
from jax import lax
from jax.experimental import pallas as pl
from jax.experimental.pallas import tpu as pltpu

F32 = jnp.float32
BF16 = jnp.bfloat16

EPS = 1e-6
LOG2E = math.log2(math.e)
GLA_CHUNK = 64
GLA_SUB = 8
GLA_CHUNKS_PER_STEP = 2
GLA_TAU = 16.0
FLASH_ROW_CHUNK = 32
FLASH_HEADS_PER_STEP = 4
LANES = 128
SUBLANES = 8
BF16_ROWS = 2 * SUBLANES
CONV_HALO = 32
VMEM_LIMIT_BYTES = 56 * 1024 * 1024


def _tile(n, pref):
    if n <= pref:
        return n
    t = pref - pref % 8
    while t >= 8:
        if n % t == 0:
            return t
        t -= 8
    return n


def _params(*sem):
    return pltpu.CompilerParams(dimension_semantics=sem, vmem_limit_bytes=VMEM_LIMIT_BYTES)


def _log_sigmoid(x):
    return jnp.minimum(x, 0.0) - jnp.log(1.0 + jnp.exp(-jnp.abs(x)))


def _rms_bf16(x, g):
    ms = jnp.mean(x * x, axis=-1, keepdims=True)
    return (x * lax.rsqrt(ms + EPS) * g).astype(BF16)


def _dot(a, b):
    return jnp.dot(a, b, preferred_element_type=F32)


def _dot_nt(a, b):
    return lax.dot_general(a, b, (((1,), (1,)), ((), ())), preferred_element_type=F32)


def _dot_tn(a, b):
    return lax.dot_general(a, b, (((0,), (0,)), ((), ())), preferred_element_type=F32)


def _conv_in_kernel(x_ref, g_ref, wa_ref, wb_ref, o_ref, h_scr):
    @pl.when(pl.program_id(1) == 0)
    def _():
        h_scr[...] = _rms_bf16(x_ref[...], g_ref[...])

    h = h_scr[...]
    o_ref[...] = _dot(h, wa_ref[...]) * jax.nn.sigmoid(_dot(h, wb_ref[...]))


def _conv_in(x, g, w_in):
    m, d = x.shape
    n = w_in.shape[1] // 2
    tm = _tile(m, 1024)
    tn = _tile(n, 512)
    nb = n // tn
    return pl.pallas_call(
        _conv_in_kernel,
        grid=(m // tm, nb),
        in_specs=[pl.BlockSpec((tm, d), lambda i, j: (i, 0)),
                  pl.BlockSpec((1, d), lambda i, j: (0, 0)),
                  pl.BlockSpec((d, tn), lambda i, j: (0, j)),
                  pl.BlockSpec((d, tn), lambda i, j: (0, j + nb))],
        out_specs=pl.BlockSpec((tm, tn), lambda i, j: (i, j)),
        out_shape=jax.ShapeDtypeStruct((m, n), F32),
        scratch_shapes=[pltpu.VMEM((tm, d), BF16)],
        compiler_params=_params("parallel", "arbitrary"),
        name="conv_in_glu",
    )(x, g.reshape(1, d), w_in, w_in)


def _fox_proj_kernel(x_ref, g_ref, w_ref, gain_ref, wf_ref, bf_ref, q_ref, k_ref, v_ref, k32_ref, v32_ref, lf_ref,
                     h_scr, *, nq, heads_per_tile, head_dim):
    j = pl.program_id(1)

    @pl.when(j == 0)
    def _():
        h = _rms_bf16(x_ref[...], g_ref[...])
        h_scr[...] = h
        lf_ref[...] = _log_sigmoid(_dot(h, wf_ref[...]) + bf_ref[...])

    acc = _dot(h_scr[...], w_ref[...])

    def head_norm(h):
        sl = slice(h * head_dim, (h + 1) * head_dim)
        x = acc[:, sl]
        ms = jnp.mean(x * x, axis=-1, keepdims=True)
        return sl, x * lax.rsqrt(ms + EPS) * gain_ref[:, sl]

    @pl.when(j < nq)
    def _():
        for h in range(heads_per_tile):
            sl, y = head_norm(h)
            q_ref[:, sl] = y.astype(q_ref.dtype)

    @pl.when((j >= nq) & (j < 2 * nq))
    def _():
        for h in range(heads_per_tile):
            sl, y = head_norm(h)
            k_ref[:, sl] = y.astype(k_ref.dtype)
            k32_ref[:, sl] = y

    @pl.when(j >= 2 * nq)
    def _():
        v_ref[...] = acc.astype(v_ref.dtype)
        v32_ref[...] = acc


def _fox_proj(x, g, w_qkv, gain_row, w_fp, b_fp, heads):
    m, d = x.shape
    n = w_qkv.shape[1]
    head_dim = d // heads
    tm = _tile(m, 1024)
    tn = _tile(d, 512)
    nq = d // tn
    assert tn % head_dim == 0 and n == 3 * d

    def cols(first):
        return lambda i, j: (i, jnp.clip(j - first, 0, nq - 1))

    return pl.pallas_call(
        functools.partial(_fox_proj_kernel, nq=nq, heads_per_tile=tn // head_dim, head_dim=head_dim),
        grid=(m // tm, n // tn),
        in_specs=[pl.BlockSpec((tm, d), lambda i, j: (i, 0)),
                  pl.BlockSpec((1, d), lambda i, j: (0, 0)),
                  pl.BlockSpec((d, tn), lambda i, j: (0, j)),
                  pl.BlockSpec((1, tn), lambda i, j: (0, j)),
                  pl.BlockSpec((d, LANES), lambda i, j: (0, 0)),
                  pl.BlockSpec((1, LANES), lambda i, j: (0, 0))],
        out_specs=[pl.BlockSpec((tm, tn), cols(0)),
                   pl.BlockSpec((tm, tn), cols(nq)),
                   pl.BlockSpec((tm, tn), cols(2 * nq)),
                   pl.BlockSpec((tm, tn), cols(nq)),
                   pl.BlockSpec((tm, tn), cols(2 * nq)),
                   pl.BlockSpec((tm, LANES), lambda i, j: (i, 0))],
        out_shape=[jax.ShapeDtypeStruct((m, d), BF16), jax.ShapeDtypeStruct((m, d), BF16),
                   jax.ShapeDtypeStruct((m, d), BF16), jax.ShapeDtypeStruct((m, d), F32),
                   jax.ShapeDtypeStruct((m, d), F32), jax.ShapeDtypeStruct((m, LANES), F32)],
        scratch_shapes=[pltpu.VMEM((tm, d), BF16)],
        compiler_params=_params("parallel", "arbitrary"),
        name="fox_proj",
    )(x, g.reshape(1, d), w_qkv, gain_row, w_fp, b_fp)


def _gla_proj_kernel(x_ref, g_ref, w_ref, w1_ref, w2_ref, ba_ref, o_ref, la_ref, h_scr):
    @pl.when(pl.program_id(1) == 0)
    def _():
        h = _rms_bf16(x_ref[...], g_ref[...])
        h_scr[...] = h
        low = _dot(h, w1_ref[...]).astype(BF16)
        la_ref[...] = _log_sigmoid(_dot(low, w2_ref[...]) + ba_ref[...]) * (1.0 / GLA_TAU)

    o_ref[...] = _dot(h_scr[...], w_ref[...])


def _gla_proj(x, g, w_qkvr, w_a1p, w_a2p, b_a):
    m, d = x.shape
    n = w_qkvr.shape[1]
    na = w_a2p.shape[1]
    tm = _tile(m, 1024)
    tn = _tile(n, 1024)
    return pl.pallas_call(
        _gla_proj_kernel,
        grid=(m // tm, n // tn),
        in_specs=[pl.BlockSpec((tm, d), lambda i, j: (i, 0)),
                  pl.BlockSpec((1, d), lambda i, j: (0, 0)),
                  pl.BlockSpec((d, tn), lambda i, j: (0, j)),
                  pl.BlockSpec(w_a1p.shape, lambda i, j: (0, 0)),
                  pl.BlockSpec(w_a2p.shape, lambda i, j: (0, 0)),
                  pl.BlockSpec((1, na), lambda i, j: (0, 0))],
        out_specs=[pl.BlockSpec((tm, tn), lambda i, j: (i, j)),
                   pl.BlockSpec((tm, na), lambda i, j: (i, 0))],
        out_shape=[jax.ShapeDtypeStruct((m, n), F32), jax.ShapeDtypeStruct((m, na), F32)],
        scratch_shapes=[pltpu.VMEM((tm, d), BF16)],
        compiler_params=_params("parallel", "arbitrary"),
        name="gla_proj",
    )(x, g.reshape(1, d), w_qkvr, w_a1p, w_a2p, b_a.reshape(1, na))


def _out_proj_kernel(a_ref, w_ref, r_ref, o_ref, *, nc):
    a = a_ref[...]
    cw = o_ref.shape[1] // nc
    for c in range(nc):
        cs = slice(c * cw, (c + 1) * cw)
        o_ref[:, cs] = r_ref[:, cs] + _dot(a, w_ref[:, cs])


def _out_proj(a, w, res):
    m, k = a.shape
    n = w.shape[1]
    tm = _tile(m, 512)
    nc = max(1, n // 512)
    return pl.pallas_call(
        functools.partial(_out_proj_kernel, nc=nc),
        grid=(m // tm,),
        in_specs=[pl.BlockSpec((tm, k), lambda i: (i, 0)),
                  pl.BlockSpec((k, n), lambda i: (0, 0)),
                  pl.BlockSpec((tm, n), lambda i: (i, 0))],
        out_specs=pl.BlockSpec((tm, n), lambda i: (i, 0)),
        out_shape=jax.ShapeDtypeStruct((m, n), F32),
        compiler_params=_params("parallel"),
        name="out_proj",
    )(a, w, res)


def _ffn_kernel(x_ref, g_ref, wu_ref, wd_ref, o_ref, h_scr):
    @pl.when(pl.program_id(1) == 0)
    def _():
        x = x_ref[...]
        h_scr[...] = _rms_bf16(x, g_ref[...])
        o_ref[...] = x

    u = jnp.maximum(_dot(h_scr[...], wu_ref[...]), 0.0)
    u = (u * u).astype(BF16)
    d = o_ref.shape[1]
    dc = min(d, 512)
    for c in range(d // dc):
        cs = slice(c * dc, (c + 1) * dc)
        o_ref[:, cs] += _dot(u, wd_ref[:, cs])


def _ffn(x, g, wu, wd):
    m, d = x.shape
    f = wu.shape[1]
    tm = _tile(m, 1024)
    tf = _tile(f, 512)
    return pl.pallas_call(
        _ffn_kernel,
        grid=(m // tm, f // tf),
        in_specs=[pl.BlockSpec((tm, d), lambda i, j: (i, 0)),
                  pl.BlockSpec((1, d), lambda i, j: (0, 0)),
                  pl.BlockSpec((d, tf), lambda i, j: (0, j)),
                  pl.BlockSpec((tf, d), lambda i, j: (j, 0))],
        out_specs=pl.BlockSpec((tm, d), lambda i, j: (i, 0)),
        out_shape=jax.ShapeDtypeStruct((m, d), F32),
        scratch_shapes=[pltpu.VMEM((tm, d), BF16)],
        compiler_params=_params("parallel", "arbitrary"),
        name="ffn",
    )(x, g.reshape(1, d), wu, wd)


def _shift_rows_up(x, r):
    n, c = x.shape
    x3 = x.reshape(n // SUBLANES, SUBLANES, c)
    rot = pltpu.roll(x3, SUBLANES - r, axis=1)
    sub = lax.broadcasted_iota(jnp.int32, (n // SUBLANES - 1, SUBLANES, c), 1)
    return jnp.where(sub < SUBLANES - r, rot[:-1], rot[1:]).reshape(n - SUBLANES, c)


def _conv_kernel(u_ref, up_ref, hist_ref, w_ref, g_ref, o_ref, ext_scr, y_scr, *, tm, width, rb, lc):
    i = pl.program_id(1)
    d = u_ref.shape[-1]
    nr = BF16_ROWS

    @pl.when(i == 0)
    def _():
        ext_scr[0:CONV_HALO, :] = hist_ref[...]

    @pl.when(i > 0)
    def _():
        ext_scr[0:CONV_HALO, :] = up_ref[...]

    ext_scr[CONV_HALO:CONV_HALO + tm, :] = u_ref[...]
    off = CONV_HALO - (width - 1)

    def row_block(r, carry):
        r0 = pl.multiple_of(r * rb, rb)
        for c in range(d // lc):
            cs = slice(c * lc, (c + 1) * lc)
            acc = jnp.zeros((rb, lc), F32)
            for rem in range(SUBLANES):
                part = None
                for k in range(width):
                    if (off + k) % SUBLANES != rem:
                        continue
                    base = (off + k) // SUBLANES * SUBLANES
                    rows = rb + SUBLANES if rem else rb
                    win = ext_scr[pl.ds(r0 + base, rows), cs].reshape(rows // SUBLANES, SUBLANES, lc)
                    term = (win * w_ref[k * SUBLANES:(k + 1) * SUBLANES, cs][None]).reshape(rows, lc)
                    part = term if part is None else part + term
                if part is None:
                    continue
                acc = acc + (_shift_rows_up(part, rem) if rem else part)
            y_scr[pl.ds(r0, rb), cs] = acc
        for s in range(rb // nr):
            rows = pl.ds(r0 + s * nr, nr)
            y = y_scr[rows, :]
            ms = jnp.mean(y * y, axis=-1, keepdims=True)
            z = y * lax.rsqrt(ms + EPS) * g_ref[...]
            o_ref[rows, :] = (z * jax.nn.sigmoid(z)).astype(o_ref.dtype)
        return carry

    lax.fori_loop(0, tm // rb, row_block, 0)


def _conv_norm_silu(u, hist_pad, w_dw, g):
    nb, nt, d = u.shape
    width = w_dw.shape[0]
    assert width - 1 <= CONV_HALO
    tm = _tile(nt, 512)
    assert tm % CONV_HALO == 0
    rb = 64 if tm % 64 == 0 else CONV_HALO
    lc = min(256, d)
    hb = tm // CONV_HALO
    w_rep = jnp.repeat(w_dw, SUBLANES, axis=0)
    return pl.pallas_call(
        functools.partial(_conv_kernel, tm=tm, width=width, rb=rb, lc=lc),
        grid=(nb, nt // tm),
        in_specs=[pl.BlockSpec((None, tm, d), lambda b, i: (b, i, 0)),
                  pl.BlockSpec((None, CONV_HALO, d), lambda b, i: (b, jnp.maximum(i * hb - 1, 0), 0)),
                  pl.BlockSpec((None, CONV_HALO, d), lambda b, i: (b, 0, 0)),
                  pl.BlockSpec((width * SUBLANES, d), lambda b, i: (0, 0)),
                  pl.BlockSpec((1, d), lambda b, i: (0, 0))],
        out_specs=pl.BlockSpec((None, tm, d), lambda b, i: (b, i, 0)),
        out_shape=jax.ShapeDtypeStruct((nb, nt, d), BF16),
        scratch_shapes=[pltpu.VMEM((CONV_HALO + tm, d), F32), pltpu.VMEM((tm, d), F32)],
        compiler_params=_params("parallel", "arbitrary"),
        name="conv_norm_silu",
    )(u, u, hist_pad, w_rep, g.reshape(1, d))


def _cumsum_kernel(x_ref, o_ref, *, n, sub_last, scale):
    x = x_ref[...]
    lane = lax.broadcasted_iota(jnp.int32, x.shape, 1)
    s = 1
    while s < n:
        x = x + jnp.where(lane >= s, pltpu.roll(x, s, axis=1), 0.0)
        s *= 2
    if sub_last:
        x = x - x[:, n - 1:n]
    o_ref[...] = x * scale


def _cumsum_lanes(x, *, sub_last=False, scale=LOG2E):
    nb, h, n = x.shape
    return pl.pallas_call(
        functools.partial(_cumsum_kernel, n=n, sub_last=sub_last, scale=scale),
        grid=(nb,),
        in_specs=[pl.BlockSpec((None, h, n), lambda b: (b, 0, 0))],
        out_specs=pl.BlockSpec((None, h, n), lambda b: (b, 0, 0)),
        out_shape=jax.ShapeDtypeStruct((nb, h, n), F32),
        compiler_params=_params("parallel"),
        name="cumsum_logf",
    )(x)


def _flash_kernel(q_ref, k_ref, v_ref, c_ref, o_ref, s_scr, p_scr, m_scr, l_scr, acc_scr, *, tq, hd, hb, rc):
    qi = pl.program_id(2)
    heads = [slice(h * hd, (h + 1) * hd) for h in range(hb)]
    qs = [q_ref[:, hs] for hs in heads]
    m_scr[...] = jnp.full(m_scr.shape, -jnp.inf, F32)
    l_scr[...] = jnp.zeros(l_scr.shape, F32)
    acc_scr[...] = jnp.zeros(acc_scr.shape, F32)

    def step(kj, masked):
        k0 = pl.multiple_of(kj * tq, tq)
        for h, hs in enumerate(heads):
            s_scr[h] = _dot_nt(qs[h], k_ref[pl.ds(k0, tq), hs])
            c_row = c_ref[h:h + 1, pl.ds(k0, tq)]
            m_all = m_scr[h]
            l_all = l_scr[h]
            m_out, l_out, a_out = [], [], []
            for r in range(tq // rc):
                rows = slice(r * rc, (r + 1) * rc)
                nvis = min(tq, (r + 1) * rc + LANES - 1) // LANES if masked else tq // LANES
                blocks = []
                for i in range(nvis):
                    cols = slice(i * LANES, (i + 1) * LANES)
                    blk = s_scr[h, rows, cols] - c_row[:, cols]
                    if masked and (i + 1) * LANES - 1 > r * rc:
                        row = lax.broadcasted_iota(jnp.int32, (rc, LANES), 0) + r * rc
                        col = lax.broadcasted_iota(jnp.int32, (rc, LANES), 1) + i * LANES
                        blk = jnp.where(col <= row, blk, -jnp.inf)
                    blocks.append(blk)
                m_old = m_all[rows]
                m_new = jnp.maximum(m_old, jnp.max(functools.reduce(jnp.maximum, blocks), axis=-1, keepdims=True))
                alpha = jnp.exp2(m_old - m_new)
                ps = [jnp.exp2(blk - m_new) for blk in blocks]
                l_out.append(alpha * l_all[rows] + jnp.sum(functools.reduce(jnp.add, ps), axis=-1, keepdims=True))
                m_out.append(m_new)
                a_out.append(alpha)
                for i in range(tq // LANES):
                    pb = ps[i].astype(BF16) if i < nvis else jnp.zeros((rc, LANES), BF16)
                    p_scr[h, rows, i * LANES:(i + 1) * LANES] = pb
            m_scr[h] = jnp.concatenate(m_out, axis=0)
            l_scr[h] = jnp.concatenate(l_out, axis=0)
            alpha = jnp.concatenate(a_out, axis=0)
            acc_scr[h] = alpha * acc_scr[h] + _dot(p_scr[h], v_ref[pl.ds(k0, tq), hs])

    def body(kj, carry):
        step(kj, False)
        return carry

    lax.fori_loop(0, qi, body, 0)
    step(qi, True)
    for h, hs in enumerate(heads):
        o_ref[:, hs] = (acc_scr[h] / l_scr[h]).astype(o_ref.dtype)


def _flash_attention(q, k, v, c, heads, hb):
    nb, nt, d = q.shape
    hd = d // heads
    assert hd == LANES
    tq = _tile(nt, 512)
    w = hb * hd
    nhb = heads // hb
    return pl.pallas_call(
        functools.partial(_flash_kernel, tq=tq, hd=hd, hb=hb, rc=min(FLASH_ROW_CHUNK, tq)),
        grid=(nb, nhb, nt // tq),
        in_specs=[pl.BlockSpec((None, tq, w), lambda b, h, i: (b, i, h)),
                  pl.BlockSpec((None, nt, w), lambda b, h, i: (b, 0, h)),
                  pl.BlockSpec((None, nt, w), lambda b, h, i: (b, 0, h)),
                  pl.BlockSpec((None, None, hb, nt), lambda b, h, i: (b, h, 0, 0))],
        out_specs=pl.BlockSpec((None, tq, w), lambda b, h, i: (b, i, h)),
        out_shape=jax.ShapeDtypeStruct((nb, nt, d), BF16),
        scratch_shapes=[pltpu.VMEM((hb, tq, tq), F32), pltpu.VMEM((hb, tq, tq), BF16),
                        pltpu.VMEM((hb, tq, LANES), F32), pltpu.VMEM((hb, tq, LANES), F32),
                        pltpu.VMEM((hb, tq, hd), F32)],
        compiler_params=_params("parallel", "parallel", "arbitrary"),
        name="fox_flash",
    )(q, k, v, c)


def _decode_kernel(q_ref, kn_ref, vn_ref, kc_ref, vc_ref, cp_ref, cn_ref, o_ref, *, hd, hb):
    ts = q_ref.shape[0]
    row = lax.broadcasted_iota(jnp.int32, (ts, ts), 0)
    col = lax.broadcasted_iota(jnp.int32, (ts, ts), 1)
    for h in range(hb):
        hs = slice(h * hd, (h + 1) * hd)
        q = q_ref[:, hs]
        s_p = _dot_nt(q, kc_ref[:, hs].astype(BF16)) - cp_ref[h:h + 1, :]
        s_n = _dot_nt(q, kn_ref[:, hs]) - cn_ref[h:h + 1, :]
        s_n = jnp.where(col <= row, s_n, -jnp.inf)
        m = jnp.maximum(jnp.max(s_p, axis=-1, keepdims=True), jnp.max(s_n, axis=-1, keepdims=True))
        p_p = jnp.exp2(s_p - m)
        p_n = jnp.exp2(s_n - m)
        l = jnp.sum(p_p, axis=-1, keepdims=True) + jnp.sum(p_n, axis=-1, keepdims=True)
        acc = (_dot(p_p.astype(BF16), vc_ref[:, hs].astype(BF16))
               + _dot(p_n.astype(BF16), vn_ref[:, hs]))
        o_ref[:, hs] = (acc / l).astype(o_ref.dtype)


def _decode_attention(q, k_new, v_new, k_cache, v_cache, c_past, c_new, heads, hb):
    nb, ts, d = q.shape
    p = k_cache.shape[1]
    hd = d // heads
    w = hb * hd
    return pl.pallas_call(
        functools.partial(_decode_kernel, hd=hd, hb=hb),
        grid=(nb, heads // hb),
        in_specs=[pl.BlockSpec((None, ts, w), lambda b, h: (b, 0, h)),
                  pl.BlockSpec((None, ts, w), lambda b, h: (b, 0, h)),
                  pl.BlockSpec((None, ts, w), lambda b, h: (b, 0, h)),
                  pl.BlockSpec((None, p, w), lambda b, h: (b, 0, h)),
                  pl.BlockSpec((None, p, w), lambda b, h: (b, 0, h)),
                  pl.BlockSpec((None, None, hb, p), lambda b, h: (b, h, 0, 0)),
                  pl.BlockSpec((None, None, hb, ts), lambda b, h: (b, h, 0, 0))],
        out_specs=pl.BlockSpec((None, ts, w), lambda b, h: (b, 0, h)),
        out_shape=jax.ShapeDtypeStruct((nb, ts, d), BF16),
        compiler_params=_params("parallel", "parallel"),
        name="fox_decode",
    )(q, k_new, v_new, k_cache, v_cache, c_past, c_new)


def _gla_chunk(q, k, v, la2, st, chunk, sub):
    row = lax.broadcasted_iota(jnp.int32, q.shape, 0)
    b = la2
    s = 1
    while s < chunk:
        b = b + jnp.where(row >= s, pltpu.roll(b, s, axis=0), 0.0)
        s *= 2
    b_last = b[chunk - 1:chunk, :]

    o = _dot_nt((q * jnp.exp2(b)).astype(BF16), st.astype(BF16))

    lane = lax.broadcasted_iota(jnp.int32, (sub, chunk), 1)
    srow = lax.broadcasted_iota(jnp.int32, (sub, chunk), 0)
    blocks = []
    for i in range(chunk // sub):
        r0 = i * sub
        qi = q[r0:r0 + sub]
        bi = b[r0:r0 + sub]
        diag = jnp.zeros((sub, chunk), F32)
        for j in range(sub):
            kj = k[r0 + j:r0 + j + 1]
            bj = b[r0 + j:r0 + j + 1]
            col = jnp.sum(qi * kj * jnp.exp2(bi - bj), axis=-1, keepdims=True)
            diag = jnp.where(lane == r0 + j, col, diag)
        a_i = jnp.where(lane - r0 <= srow, diag, 0.0)
        if i > 0:
            ref_b = b[r0:r0 + 1]
            qg = (qi * jnp.exp2(bi - ref_b)).astype(BF16)
            rr = min(chunk, -(-r0 // BF16_ROWS) * BF16_ROWS)
            kg = (k[:rr] * jnp.exp2(jnp.minimum(ref_b - b[:rr], 0.0))).astype(BF16)
            if rr < chunk:
                kg = jnp.concatenate([kg, jnp.zeros((chunk - rr, kg.shape[1]), BF16)], axis=0)
            a_i = jnp.where(lane < r0, _dot_nt(qg, kg), a_i)
        blocks.append(a_i)
    a = jnp.concatenate(blocks, axis=0)
    o = o + _dot(a.astype(BF16), v)

    kd = (k * jnp.exp2(b_last - b)).astype(BF16)
    st_new = st * jnp.exp2(b_last) + _dot_tn(v, kd)
    return o, st_new


def _gla_kernel(q_ref, k_ref, v_ref, la_ref, s0_ref, o_ref, s_out_ref, s_scr, *, chunk, sub, cps, nsteps, scale):
    step = pl.program_id(2)

    @pl.when(step == 0)
    def _():
        s_scr[...] = s0_ref[...]

    st = s_scr[...]
    for ci in range(cps):
        rows = slice(ci * chunk, (ci + 1) * chunk)
        o, st = _gla_chunk(q_ref[rows, :] * scale, k_ref[rows, :], v_ref[rows, :].astype(BF16),
                           la_ref[rows, :] * LOG2E, st, chunk, sub)
        o_ref[rows, :] = o
    s_scr[...] = st

    @pl.when(step == nsteps - 1)
    def _():
        s_out_ref[...] = st


def _gla(qkvr, la, s0t, heads, dk, dv, chunk):
    nb, nt, _ = qkvr.shape
    cps = GLA_CHUNKS_PER_STEP if nt % (GLA_CHUNKS_PER_STEP * chunk) == 0 else 1
    rows = cps * chunk
    nsteps = nt // rows
    kb = heads
    vb = 2 * heads * dk // dv
    assert (2 * heads * dk) % dv == 0
    return pl.pallas_call(
        functools.partial(_gla_kernel, chunk=chunk, sub=min(GLA_SUB, chunk), cps=cps, nsteps=nsteps, scale=dk ** -0.5),
        grid=(nb, heads, nsteps),
        in_specs=[pl.BlockSpec((None, rows, dk), lambda b, h, c: (b, c, h)),
                  pl.BlockSpec((None, rows, dk), lambda b, h, c: (b, c, kb + h)),
                  pl.BlockSpec((None, rows, dv), lambda b, h, c: (b, c, vb + h)),
                  pl.BlockSpec((None, rows, dk), lambda b, h, c: (b, c, h)),
                  pl.BlockSpec((None, None, dv, dk), lambda b, h, c: (b, h, 0, 0))],
        out_specs=[pl.BlockSpec((None, rows, dv), lambda b, h, c: (b, c, h)),
                   pl.BlockSpec((None, None, dv, dk), lambda b, h, c: (b, h, 0, 0))],
        out_shape=[jax.ShapeDtypeStruct((nb, nt, heads * dv), F32),
                   jax.ShapeDtypeStruct((nb, heads, dv, dk), F32)],
        scratch_shapes=[pltpu.VMEM((dv, dk), F32)],
        compiler_params=_params("parallel", "parallel", "arbitrary"),
        name="gla_chunk",
    )(qkvr, qkvr, qkvr, la, s0t)


def _gla_out_kernel(o_ref, r_ref, g_ref, a_ref, *, heads, dv):
    g = g_ref[...]
    for h in range(heads):
        sl = slice(h * dv, (h + 1) * dv)
        x = o_ref[:, sl]
        r = r_ref[:, sl]
        ms = jnp.mean(x * x, axis=-1, keepdims=True)
        a_ref[:, sl] = (x * lax.rsqrt(ms + EPS) * g * (r * jax.nn.sigmoid(r))).astype(a_ref.dtype)


def _gla_out(o, qkvr, g_o, heads, dv):
    m, d = o.shape
    tm = _tile(m, 512)
    rb = qkvr.shape[1] // d - 1
    return pl.pallas_call(
        functools.partial(_gla_out_kernel, heads=heads, dv=dv),
        grid=(m // tm,),
        in_specs=[pl.BlockSpec((tm, d), lambda i: (i, 0)),
                  pl.BlockSpec((tm, d), lambda i: (i, rb)),
                  pl.BlockSpec((1, dv), lambda i: (0, 0))],
        out_specs=pl.BlockSpec((tm, d), lambda i: (i, 0)),
        out_shape=jax.ShapeDtypeStruct((m, d), BF16),
        compiler_params=_params("parallel"),
        name="gla_out_norm",
    )(o, qkvr, g_o.reshape(1, dv))


def _conv_layer(x, nb, nt, hist, g_mix, w_in, w_dw, g_norm, w_out):
    m, d = x.shape
    width = w_dw.shape[0]
    u = _conv_in(x, g_mix, w_in)
    u3 = u.reshape(nb, nt, d)
    if hist is None:
        hist_pad = jnp.zeros((nb, CONV_HALO, d), F32)
        tail = u3[:, nt - (width - 1):]
    else:
        hist_pad = jnp.pad(hist, ((0, 0), (CONV_HALO - (width - 1), 0), (0, 0)))
        tail = jnp.concatenate([hist, u3], axis=1)[:, nt:]
    a = _conv_norm_silu(u3, hist_pad, w_dw, g_norm)
    return _out_proj(a.reshape(m, d), w_out, x), tail


def _pad_cols(w, n):
    return jnp.pad(w, ((0, 0), (0, n - w.shape[1])))


def _fox_layer(xp, xs, dims, cache_k, cache_v, cache_logf, g_mix, w_qkv, w_f, b_f, g_q, g_k, w_o):
    nb, nt, nbs, nts = dims
    d = xp.shape[1]
    heads, hd = cache_k.shape[2], cache_k.shape[3]
    past = cache_k.shape[1]
    hb = FLASH_HEADS_PER_STEP if heads % FLASH_HEADS_PER_STEP == 0 else 1
    w_qkv = w_qkv.astype(BF16)
    w_o = w_o.astype(BF16)
    w_fp = _pad_cols(w_f, LANES).astype(BF16)
    b_fp = jnp.pad(b_f, (0, LANES - heads)).reshape(1, LANES)
    gain_row = jnp.concatenate([jnp.tile(g_q, heads) * (hd ** -0.5 * LOG2E), jnp.tile(g_k, heads),
                                jnp.ones((d,), F32)]).reshape(1, 3 * d)

    def project(x, b_, t_):
        q, k, v, k32, v32, lf = _fox_proj(x, g_mix, w_qkv, gain_row, w_fp, b_fp, heads)
        lf = lf[:, :heads].reshape(b_, t_, heads)
        new = (k32.reshape(b_, t_, heads, hd), v32.reshape(b_, t_, heads, hd), lf)
        return q.reshape(b_, t_, d), k.reshape(b_, t_, d), v.reshape(b_, t_, d), jnp.transpose(lf, (0, 2, 1)), new

    q, k, v, lf_t, new_p = project(xp, nb, nt)
    c = _cumsum_lanes(lf_t)
    o = _flash_attention(q, k, v, c.reshape(nb, heads // hb, hb, nt), heads, hb)
    yp = _out_proj(o.reshape(nb * nt, d), w_o, xp)

    qs, ks, vs, lfs_t, new_s = project(xs, nbs, nts)
    c_past = _cumsum_lanes(jnp.transpose(cache_logf, (0, 2, 1)), sub_last=True)
    nts_pad = -(-nts // LANES) * LANES
    c_new = _cumsum_lanes(jnp.pad(lfs_t, ((0, 0), (0, 0), (0, nts_pad - nts))))[:, :, :nts]
    os_ = _decode_attention(qs, ks, vs, cache_k.reshape(nbs, past, d), cache_v.reshape(nbs, past, d),
                            c_past.reshape(nbs, heads // hb, hb, past), c_new.reshape(nbs, heads // hb, hb, nts),
                            heads, hb)
    ys = _out_proj(os_.reshape(nbs * nts, d), w_o, xs)
    return yp, ys, new_p + new_s


def _gla_group(x, nb, nt, s0t, chunk, heads, dk, dv, g_mix, w_qkvr, w_a1p, w_a2p, b_a, g_o, w_o):
    m, d = x.shape
    qkvr, la = _gla_proj(x, g_mix, w_qkvr, w_a1p, w_a2p, b_a)
    n = qkvr.shape[1]
    o, st = _gla(qkvr.reshape(nb, nt, n), la.reshape(nb, nt, heads * dk), s0t, heads, dk, dv, chunk)
    a = _gla_out(o.reshape(m, d), qkvr, g_o, heads, dv)
    return _out_proj(a, w_o, x), jnp.swapaxes(st, 2, 3)


def _gla_layer(xp, xs, dims, state, g_mix, w_qkvr, w_a1, w_a2, b_a, g_o, w_o):
    nb, nt, nbs, nts = dims
    heads, dk, dv = state.shape[1], state.shape[2], state.shape[3]
    rank = w_a1.shape[1]
    w_qkvr = w_qkvr.astype(BF16)
    w_o = w_o.astype(BF16)
    w_a1p = _pad_cols(w_a1, LANES).astype(BF16)
    w_a2p = jnp.pad(w_a2, ((0, LANES - rank), (0, 0))).astype(BF16)
    common = (heads, dk, dv, g_mix, w_qkvr, w_a1p, w_a2p, b_a, g_o, w_o)
    zeros = jnp.zeros((nb, heads, dv, dk), F32)
    yp, sp = _gla_group(xp, nb, nt, zeros, GLA_CHUNK, *common)
    ys, ss = _gla_group(xs, nbs, nts, jnp.swapaxes(state, 2, 3), nts, *common)
    return yp, ys, (sp, ss)


def kernel(x_prompt, x_sample, cache_conv_l0, cache_k_l1, cache_v_l1, cache_logf_l1, state_gla_l2, cache_conv_l3, norm_mix_l0, conv_w_in_l0, conv_w_dw_l0, conv_norm_l0, conv_w_out_l0, norm_ffn_l0, ffn_w_up_l0, ffn_w_down_l0, norm_mix_l1, fox_w_qkv_l1, fox_w_f_l1, fox_b_f_l1, fox_q_norm_l1, fox_k_norm_l1, fox_w_o_l1, norm_ffn_l1, ffn_w_up_l1, ffn_w_down_l1, norm_mix_l2, gla_w_qkvr_l2, gla_w_a1_l2, gla_w_a2_l2, gla_b_a_l2, gla_o_norm_l2, gla_w_o_l2, norm_ffn_l2, ffn_w_up_l2, ffn_w_down_l2, norm_mix_l3, conv_w_in_l3, conv_w_dw_l3, conv_norm_l3, conv_w_out_l3, norm_ffn_l3, ffn_w_up_l3, ffn_w_down_l3):
    nb, nt, d = x_prompt.shape
    nbs, nts, _ = x_sample.shape
    dims = (nb, nt, nbs, nts)
    xp = x_prompt.reshape(nb * nt, d)
    xs = x_sample.reshape(nbs * nts, d)

    def ffn(xp, xs, g, wu, wd):
        wu = wu.astype(BF16)
        wd = wd.astype(BF16)
        return _ffn(xp, g, wu, wd), _ffn(xs, g, wu, wd)

    def conv(xp, xs, cache, g_mix, w_in, w_dw, g_norm, w_out):
        w_in = w_in.astype(BF16)
        w_out = w_out.astype(BF16)
        yp, sp = _conv_layer(xp, nb, nt, None, g_mix, w_in, w_dw, g_norm, w_out)
        ys, ss = _conv_layer(xs, nbs, nts, cache, g_mix, w_in, w_dw, g_norm, w_out)
        return yp, ys, (sp, ss)

    xp, xs, st0 = conv(xp, xs, cache_conv_l0, norm_mix_l0, conv_w_in_l0, conv_w_dw_l0, conv_norm_l0, conv_w_out_l0)
    xp, xs = ffn(xp, xs, norm_ffn_l0, ffn_w_up_l0, ffn_w_down_l0)
    xp, xs, st1 = _fox_layer(xp, xs, dims, cache_k_l1, cache_v_l1, cache_logf_l1, norm_mix_l1, fox_w_qkv_l1,
                             fox_w_f_l1, fox_b_f_l1, fox_q_norm_l1, fox_k_norm_l1, fox_w_o_l1)
    xp, xs = ffn(xp, xs, norm_ffn_l1, ffn_w_up_l1, ffn_w_down_l1)
    xp, xs, st2 = _gla_layer(xp, xs, dims, state_gla_l2, norm_mix_l2, gla_w_qkvr_l2, gla_w_a1_l2, gla_w_a2_l2,
                             gla_b_a_l2, gla_o_norm_l2, gla_w_o_l2)
    xp, xs = ffn(xp, xs, norm_ffn_l2, ffn_w_up_l2, ffn_w_down_l2)
    xp, xs, st3 = conv(xp, xs, cache_conv_l3, norm_mix_l3, conv_w_in_l3, conv_w_dw_l3, conv_norm_l3, conv_w_out_l3)
    xp, xs = ffn(xp, xs, norm_ffn_l3, ffn_w_up_l3, ffn_w_down_l3)
    return (xp.reshape(nb, nt, d), xs.reshape(nbs, nts, d), *st0, *st1, *st2, *st3)
```

```python
import functools
import math

import jax
import jax.numpy as jnp
from jax import lax
from jax.experimental import pallas as pl
from jax.experimental.pallas import tpu as pltpu

F32 = jnp.float32
BF16 = jnp.bfloat16

EPS = 1e-6
LOG2E = math.log2(math.e)
GLA_CHUNK = 64
GLA_SUB = 8
GLA_CHUNKS_PER_STEP = 4
GLA_TAU = 16.0
FLASH_ROW_CHUNK = 32
FLASH_HEADS_PER_STEP = 8
LANES = 128
SUBLANES = 8
BF16_ROWS = 2 * SUBLANES
CONV_HALO = 32
VMEM_LIMIT_BYTES = 56 * 1024 * 1024


def _tile(n, pref):
    if n <= pref:
        return n
    t = pref - pref % 8
    while t >= 8:
        if n % t == 0:
            return t
        t -= 8
    return n


def _params(*sem):
    return pltpu.CompilerParams(dimension_semantics=sem, vmem_limit_bytes=VMEM_LIMIT_BYTES)


def _log_sigmoid(x):
    return jnp.minimum(x, 0.0) - jnp.log(1.0 + jnp.exp(-jnp.abs(x)))


def _rms_bf16(x, g):
    ms = jnp.mean(x * x, axis=-1, keepdims=True)
    return (x * lax.rsqrt(ms + EPS) * g).astype(BF16)


def _dot(a, b):
    return jnp.dot(a, b, preferred_element_type=F32)


def _dot_nt(a, b):
    return lax.dot_general(a, b, (((1,), (1,)), ((), ())), preferred_element_type=F32)


def _dot_tn(a, b):
    return lax.dot_general(a, b, (((0,), (0,)), ((), ())), preferred_element_type=F32)


def _conv_in_kernel(x_ref, g_ref, wa_ref, wb_ref, o_ref, h_scr):
    @pl.when(pl.program_id(1) == 0)
    def _():
        h_scr[...] = _rms_bf16(x_ref[...], g_ref[...])

    h = h_scr[...]
    o_ref[...] = _dot(h, wa_ref[...]) * jax.nn.sigmoid(_dot(h, wb_ref[...]))


def _conv_in(x, g, w_in):
    m, d = x.shape
    n = w_in.shape[1] // 2
    tm = _tile(m, 1024)
    tn = _tile(n, 512)
    nb = n // tn
    return pl.pallas_call(
        _conv_in_kernel,
        grid=(m // tm, nb),
        in_specs=[pl.BlockSpec((tm, d), lambda i, j: (i, 0)),
                  pl.BlockSpec((1, d), lambda i, j: (0, 0)),
                  pl.BlockSpec((d, tn), lambda i, j: (0, j)),
                  pl.BlockSpec((d, tn), lambda i, j: (0, j + nb))],
        out_specs=pl.BlockSpec((tm, tn), lambda i, j: (i, j)),
        out_shape=jax.ShapeDtypeStruct((m, n), F32),
        scratch_shapes=[pltpu.VMEM((tm, d), BF16)],
        compiler_params=_params("parallel", "arbitrary"),
        name="conv_in_glu",
    )(x, g.reshape(1, d), w_in, w_in)


def _fox_proj_kernel(x_ref, g_ref, w_ref, gain_ref, wf_ref, bf_ref, q_ref, k_ref, v_ref, k32_ref, v32_ref, lf_ref,
                     h_scr, *, nq, heads_per_tile, head_dim):
    j = pl.program_id(1)

    @pl.when(j == 0)
    def _():
        h = _rms_bf16(x_ref[...], g_ref[...])
        h_scr[...] = h
        lf_ref[...] = _log_sigmoid(_dot(h, wf_ref[...]) + bf_ref[...])

    acc = _dot(h_scr[...], w_ref[...])

    def head_norm(h):
        sl = slice(h * head_dim, (h + 1) * head_dim)
        x = acc[:, sl]
        ms = jnp.mean(x * x, axis=-1, keepdims=True)
        return sl, x * lax.rsqrt(ms + EPS) * gain_ref[:, sl]

    @pl.when(j < nq)
    def _():
        for h in range(heads_per_tile):
            sl, y = head_norm(h)
            q_ref[:, sl] = y.astype(q_ref.dtype)

    @pl.when((j >= nq) & (j < 2 * nq))
    def _():
        for h in range(heads_per_tile):
            sl, y = head_norm(h)
            k_ref[:, sl] = y.astype(k_ref.dtype)
            k32_ref[:, sl] = y

    @pl.when(j >= 2 * nq)
    def _():
        v_ref[...] = acc.astype(v_ref.dtype)
        v32_ref[...] = acc


def _fox_proj(x, g, w_qkv, gain_row, w_fp, b_fp, heads):
    m, d = x.shape
    n = w_qkv.shape[1]
    head_dim = d // heads
    tm = _tile(m, 1024)
    tn = _tile(d, 512)
    nq = d // tn
    assert tn % head_dim == 0 and n == 3 * d

    def cols(first):
        return lambda i, j: (i, jnp.clip(j - first, 0, nq - 1))

    return pl.pallas_call(
        functools.partial(_fox_proj_kernel, nq=nq, heads_per_tile=tn // head_dim, head_dim=head_dim),
        grid=(m // tm, n // tn),
        in_specs=[pl.BlockSpec((tm, d), lambda i, j: (i, 0)),
                  pl.BlockSpec((1, d), lambda i, j: (0, 0)),
                  pl.BlockSpec((d, tn), lambda i, j: (0, j)),
                  pl.BlockSpec((1, tn), lambda i, j: (0, j)),
                  pl.BlockSpec((d, LANES), lambda i, j: (0, 0)),
                  pl.BlockSpec((1, LANES), lambda i, j: (0, 0))],
        out_specs=[pl.BlockSpec((tm, tn), cols(0)),
                   pl.BlockSpec((tm, tn), cols(nq)),
                   pl.BlockSpec((tm, tn), cols(2 * nq)),
                   pl.BlockSpec((tm, tn), cols(nq)),
                   pl.BlockSpec((tm, tn), cols(2 * nq)),
                   pl.BlockSpec((tm, LANES), lambda i, j: (i, 0))],
        out_shape=[jax.ShapeDtypeStruct((m, d), BF16), jax.ShapeDtypeStruct((m, d), BF16),
                   jax.ShapeDtypeStruct((m, d), BF16), jax.ShapeDtypeStruct((m, d), F32),
                   jax.ShapeDtypeStruct((m, d), F32), jax.ShapeDtypeStruct((m, LANES), F32)],
        scratch_shapes=[pltpu.VMEM((tm, d), BF16)],
        compiler_params=_params("parallel", "arbitrary"),
        name="fox_proj",
    )(x, g.reshape(1, d), w_qkv, gain_row, w_fp, b_fp)


def _gla_proj_kernel(x_ref, g_ref, w_ref, w1_ref, w2_ref, ba_ref, o_ref, la_ref, h_scr):
    @pl.when(pl.program_id(1) == 0)
    def _():
        h = _rms_bf16(x_ref[...], g_ref[...])
        h_scr[...] = h
        low = _dot(h, w1_ref[...]).astype(BF16)
        la_ref[...] = _log_sigmoid(_dot(low, w2_ref[...]) + ba_ref[...]) * (1.0 / GLA_TAU)

    o_ref[...] = _dot(h_scr[...], w_ref[...])


def _gla_proj(x, g, w_qkvr, w_a1p, w_a2p, b_a):
    m, d = x.shape
    n = w_qkvr.shape[1]
    na = w_a2p.shape[1]
    tm = _tile(m, 1024)
    tn = _tile(n, 1024)
    return pl.pallas_call(
        _gla_proj_kernel,
        grid=(m // tm, n // tn),
        in_specs=[pl.BlockSpec((tm, d), lambda i, j: (i, 0)),
                  pl.BlockSpec((1, d), lambda i, j: (0, 0)),
                  pl.BlockSpec((d, tn), lambda i, j: (0, j)),
                  pl.BlockSpec(w_a1p.shape, lambda i, j: (0, 0)),
                  pl.BlockSpec(w_a2p.shape, lambda i, j: (0, 0)),
                  pl.BlockSpec((1, na), lambda i, j: (0, 0))],
        out_specs=[pl.BlockSpec((tm, tn), lambda i, j: (i, j)),
                   pl.BlockSpec((tm, na), lambda i, j: (i, 0))],
        out_shape=[jax.ShapeDtypeStruct((m, n), F32), jax.ShapeDtypeStruct((m, na), F32)],
        scratch_shapes=[pltpu.VMEM((tm, d), BF16)],
        compiler_params=_params("parallel", "arbitrary"),
        name="gla_proj",
    )(x, g.reshape(1, d), w_qkvr, w_a1p, w_a2p, b_a.reshape(1, na))


def _out_proj_kernel(a_ref, w_ref, r_ref, o_ref, *, nc):
    a = a_ref[...]
    cw = o_ref.shape[1] // nc
    for c in range(nc):
        cs = slice(c * cw, (c + 1) * cw)
        o_ref[:, cs] = r_ref[:, cs] + _dot(a, w_ref[:, cs])


def _out_proj(a, w, res):
    m, k = a.shape
    n = w.shape[1]
    tm = _tile(m, 512)
    nc = max(1, n // 512)
    return pl.pallas_call(
        functools.partial(_out_proj_kernel, nc=nc),
        grid=(m // tm,),
        in_specs=[pl.BlockSpec((tm, k), lambda i: (i, 0)),
                  pl.BlockSpec((k, n), lambda i: (0, 0)),
                  pl.BlockSpec((tm, n), lambda i: (i, 0))],
        out_specs=pl.BlockSpec((tm, n), lambda i: (i, 0)),
        out_shape=jax.ShapeDtypeStruct((m, n), F32),
        compiler_params=_params("parallel"),
        name="out_proj",
    )(a, w, res)


def _ffn_kernel(x_ref, g_ref, wu_ref, wd_ref, o_ref, h_scr):
    @pl.when(pl.program_id(1) == 0)
    def _():
        x = x_ref[...]
        h_scr[...] = _rms_bf16(x, g_ref[...])
        o_ref[...] = x

    u = jnp.maximum(_dot(h_scr[...], wu_ref[...]), 0.0)
    u = (u * u).astype(BF16)
    d = o_ref.shape[1]
    dc = min(d, 512)
    for c in range(d // dc):
        cs = slice(c * dc, (c + 1) * dc)
        o_ref[:, cs] += _dot(u, wd_ref[:, cs])


def _ffn(x, g, wu, wd):
    m, d = x.shape
    f = wu.shape[1]
    tm = _tile(m, 1024)
    tf = _tile(f, 512 if tm > 256 else 2048)
    return pl.pallas_call(
        _ffn_kernel,
        grid=(m // tm, f // tf),
        in_specs=[pl.BlockSpec((tm, d), lambda i, j: (i, 0)),
                  pl.BlockSpec((1, d), lambda i, j: (0, 0)),
                  pl.BlockSpec((d, tf), lambda i, j: (0, j)),
                  pl.BlockSpec((tf, d), lambda i, j: (j, 0))],
        out_specs=pl.BlockSpec((tm, d), lambda i, j: (i, 0)),
        out_shape=jax.ShapeDtypeStruct((m, d), F32),
        scratch_shapes=[pltpu.VMEM((tm, d), BF16)],
        compiler_params=_params("parallel", "arbitrary"),
        name="ffn",
    )(x, g.reshape(1, d), wu, wd)


def _shift_rows_up(x, r):
    n, c = x.shape
    x3 = x.reshape(n // SUBLANES, SUBLANES, c)
    rot = pltpu.roll(x3, SUBLANES - r, axis=1)
    sub = lax.broadcasted_iota(jnp.int32, (n // SUBLANES - 1, SUBLANES, c), 1)
    return jnp.where(sub < SUBLANES - r, rot[:-1], rot[1:]).reshape(n - SUBLANES, c)


def _conv_kernel(u_ref, up_ref, hist_ref, w_ref, g_ref, o_ref, ext_scr, y_scr, *, tm, width, rb, lc):
    i = pl.program_id(1)
    d = u_ref.shape[-1]
    nr = BF16_ROWS

    @pl.when(i == 0)
    def _():
        ext_scr[0:CONV_HALO, :] = hist_ref[...]

    @pl.when(i > 0)
    def _():
        ext_scr[0:CONV_HALO, :] = up_ref[...]

    ext_scr[CONV_HALO:CONV_HALO + tm, :] = u_ref[...]
    off = CONV_HALO - (width - 1)

    def row_block(r, carry):
        r0 = pl.multiple_of(r * rb, rb)
        for c in range(d // lc):
            cs = slice(c * lc, (c + 1) * lc)
            acc = jnp.zeros((rb, lc), F32)
            for rem in range(SUBLANES):
                part = None
                for k in range(width):
                    if (off + k) % SUBLANES != rem:
                        continue
                    base = (off + k) // SUBLANES * SUBLANES
                    rows = rb + SUBLANES if rem else rb
                    win = ext_scr[pl.ds(r0 + base, rows), cs].reshape(rows // SUBLANES, SUBLANES, lc)
                    term = (win * w_ref[k * SUBLANES:(k + 1) * SUBLANES, cs][None]).reshape(rows, lc)
                    part = term if part is None else part + term
                if part is None:
                    continue
                acc = acc + (_shift_rows_up(part, rem) if rem else part)
            y_scr[pl.ds(r0, rb), cs] = acc
        for s in range(rb // nr):
            rows = pl.ds(r0 + s * nr, nr)
            y = y_scr[rows, :]
            ms = jnp.mean(y * y, axis=-1, keepdims=True)
            z = y * lax.rsqrt(ms + EPS) * g_ref[...]
            o_ref[rows, :] = (z * jax.nn.sigmoid(z)).astype(o_ref.dtype)
        return carry

    lax.fori_loop(0, tm // rb, row_block, 0)


def _conv_norm_silu(u, hist_pad, w_dw, g):
    nb, nt, d = u.shape
    width = w_dw.shape[0]
    assert width - 1 <= CONV_HALO
    tm = _tile(nt, 512)
    assert tm % CONV_HALO == 0
    rb = 64 if tm % 64 == 0 else CONV_HALO
    lc = min(256, d)
    hb = tm // CONV_HALO
    w_rep = jnp.repeat(w_dw, SUBLANES, axis=0)
    return pl.pallas_call(
        functools.partial(_conv_kernel, tm=tm, width=width, rb=rb, lc=lc),
        grid=(nb, nt // tm),
        in_specs=[pl.BlockSpec((None, tm, d), lambda b, i: (b, i, 0)),
                  pl.BlockSpec((None, CONV_HALO, d), lambda b, i: (b, jnp.maximum(i * hb - 1, 0), 0)),
                  pl.BlockSpec((None, CONV_HALO, d), lambda b, i: (b, 0, 0)),
                  pl.BlockSpec((width * SUBLANES, d), lambda b, i: (0, 0)),
                  pl.BlockSpec((1, d), lambda b, i: (0, 0))],
        out_specs=pl.BlockSpec((None, tm, d), lambda b, i: (b, i, 0)),
        out_shape=jax.ShapeDtypeStruct((nb, nt, d), BF16),
        scratch_shapes=[pltpu.VMEM((CONV_HALO + tm, d), F32), pltpu.VMEM((tm, d), F32)],
        compiler_params=_params("parallel", "arbitrary"),
        name="conv_norm_silu",
    )(u, u, hist_pad, w_rep, g.reshape(1, d))


def _cumsum_kernel(x_ref, o_ref, *, n, sub_last, scale):
    x = x_ref[...]
    lane = lax.broadcasted_iota(jnp.int32, x.shape, 1)
    s = 1
    while s < n:
        x = x + jnp.where(lane >= s, pltpu.roll(x, s, axis=1), 0.0)
        s *= 2
    if sub_last:
        x = x - x[:, n - 1:n]
    o_ref[...] = x * scale


def _cumsum_lanes(x, *, sub_last=False, scale=LOG2E):
    nb, h, n = x.shape
    return pl.pallas_call(
        functools.partial(_cumsum_kernel, n=n, sub_last=sub_last, scale=scale),
        grid=(nb,),
        in_specs=[pl.BlockSpec((None, h, n), lambda b: (b, 0, 0))],
        out_specs=pl.BlockSpec((None, h, n), lambda b: (b, 0, 0)),
        out_shape=jax.ShapeDtypeStruct((nb, h, n), F32),
        compiler_params=_params("parallel"),
        name="cumsum_logf",
    )(x)


def _flash_kernel(q_ref, k_ref, v_ref, c_ref, o_ref, s_scr, p_scr, m_scr, l_scr, acc_scr, *, tq, hd, hb, rc):
    qi = pl.program_id(2)
    heads = [slice(h * hd, (h + 1) * hd) for h in range(hb)]
    qs = [q_ref[:, hs] for hs in heads]
    m_scr[...] = jnp.full(m_scr.shape, -jnp.inf, F32)
    l_scr[...] = jnp.zeros(l_scr.shape, F32)
    acc_scr[...] = jnp.zeros(acc_scr.shape, F32)

    def step(kj, masked):
        k0 = pl.multiple_of(kj * tq, tq)
        for h, hs in enumerate(heads):
            s_scr[h] = _dot_nt(qs[h], k_ref[pl.ds(k0, tq), hs])
            c_row = c_ref[h:h + 1, pl.ds(k0, tq)]
            m_all = m_scr[h]
            l_all = l_scr[h]
            m_out, l_out, a_out = [], [], []
            for r in range(tq // rc):
                rows = slice(r * rc, (r + 1) * rc)
                nvis = min(tq, (r + 1) * rc + LANES - 1) // LANES if masked else tq // LANES
                blocks = []
                for i in range(nvis):
                    cols = slice(i * LANES, (i + 1) * LANES)
                    blk = s_scr[h, rows, cols] - c_row[:, cols]
                    if masked and (i + 1) * LANES - 1 > r * rc:
                        row = lax.broadcasted_iota(jnp.int32, (rc, LANES), 0) + r * rc
                        col = lax.broadcasted_iota(jnp.int32, (rc, LANES), 1) + i * LANES
                        blk = jnp.where(col <= row, blk, -jnp.inf)
                    blocks.append(blk)
                m_old = m_all[rows]
                m_new = jnp.maximum(m_old, jnp.max(functools.reduce(jnp.maximum, blocks), axis=-1, keepdims=True))
                alpha = jnp.exp2(m_old - m_new)
                ps = [jnp.exp2(blk - m_new) for blk in blocks]
                l_out.append(alpha * l_all[rows] + jnp.sum(functools.reduce(jnp.add, ps), axis=-1, keepdims=True))
                m_out.append(m_new)
                a_out.append(alpha)
                for i in range(tq // LANES):
                    pb = ps[i].astype(BF16) if i < nvis else jnp.zeros((rc, LANES), BF16)
                    p_scr[h, rows, i * LANES:(i + 1) * LANES] = pb
            m_scr[h] = jnp.concatenate(m_out, axis=0)
            l_scr[h] = jnp.concatenate(l_out, axis=0)
            alpha = jnp.concatenate(a_out, axis=0)
            acc_scr[h] = alpha * acc_scr[h] + _dot(p_scr[h], v_ref[pl.ds(k0, tq), hs])

    def body(kj, carry):
        step(kj, False)
        return carry

    lax.fori_loop(0, qi, body, 0)
    step(qi, True)
    for h, hs in enumerate(heads):
        o_ref[:, hs] = (acc_scr[h] / l_scr[h]).astype(o_ref.dtype)


def _flash_attention(q, k, v, c, heads, hb):
    nb, nt, d = q.shape
    hd = d // heads
    assert hd == LANES
    tq = _tile(nt, 512)
    w = hb * hd
    nhb = heads // hb
    return pl.pallas_call(
        functools.partial(_flash_kernel, tq=tq, hd=hd, hb=hb, rc=min(FLASH_ROW_CHUNK, tq)),
        grid=(nb, nhb, nt // tq),
        in_specs=[pl.BlockSpec((None, tq, w), lambda b, h, i: (b, i, h)),
                  pl.BlockSpec((None, nt, w), lambda b, h, i: (b, 0, h)),
                  pl.BlockSpec((None, nt, w), lambda b, h, i: (b, 0, h)),
                  pl.BlockSpec((None, None, hb, nt), lambda b, h, i: (b, h, 0, 0))],
        out_specs=pl.BlockSpec((None, tq, w), lambda b, h, i: (b, i, h)),
        out_shape=jax.ShapeDtypeStruct((nb, nt, d), BF16),
        scratch_shapes=[pltpu.VMEM((hb, tq, tq), F32), pltpu.VMEM((hb, tq, tq), BF16),
                        pltpu.VMEM((hb, tq, LANES), F32), pltpu.VMEM((hb, tq, LANES), F32),
                        pltpu.VMEM((hb, tq, hd), F32)],
        compiler_params=_params("parallel", "parallel", "arbitrary"),
        name="fox_flash",
    )(q, k, v, c)


def _decode_kernel(q_ref, kn_ref, vn_ref, kc_ref, vc_ref, cp_ref, cn_ref, o_ref, *, hd, hb):
    ts = q_ref.shape[0]
    row = lax.broadcasted_iota(jnp.int32, (ts, ts), 0)
    col = lax.broadcasted_iota(jnp.int32, (ts, ts), 1)
    for h in range(hb):
        hs = slice(h * hd, (h + 1) * hd)
        q = q_ref[:, hs]
        s_p = _dot_nt(q, kc_ref[:, hs].astype(BF16)) - cp_ref[h:h + 1, :]
        s_n = _dot_nt(q, kn_ref[:, hs]) - cn_ref[h:h + 1, :]
        s_n = jnp.where(col <= row, s_n, -jnp.inf)
        m = jnp.maximum(jnp.max(s_p, axis=-1, keepdims=True), jnp.max(s_n, axis=-1, keepdims=True))
        p_p = jnp.exp2(s_p - m)
        p_n = jnp.exp2(s_n - m)
        l = jnp.sum(p_p, axis=-1, keepdims=True) + jnp.sum(p_n, axis=-1, keepdims=True)
        acc = (_dot(p_p.astype(BF16), vc_ref[:, hs].astype(BF16))
               + _dot(p_n.astype(BF16), vn_ref[:, hs]))
        o_ref[:, hs] = (acc / l).astype(o_ref.dtype)


def _decode_attention(q, k_new, v_new, k_cache, v_cache, c_past, c_new, heads, hb):
    nb, ts, d = q.shape
    p = k_cache.shape[1]
    hd = d // heads
    w = hb * hd
    return pl.pallas_call(
        functools.partial(_decode_kernel, hd=hd, hb=hb),
        grid=(nb, heads // hb),
        in_specs=[pl.BlockSpec((None, ts, w), lambda b, h: (b, 0, h)),
                  pl.BlockSpec((None, ts, w), lambda b, h: (b, 0, h)),
                  pl.BlockSpec((None, ts, w), lambda b, h: (b, 0, h)),
                  pl.BlockSpec((None, p, w), lambda b, h: (b, 0, h)),
                  pl.BlockSpec((None, p, w), lambda b, h: (b, 0, h)),
                  pl.BlockSpec((None, None, hb, p), lambda b, h: (b, h, 0, 0)),
                  pl.BlockSpec((None, None, hb, ts), lambda b, h: (b, h, 0, 0))],
        out_specs=pl.BlockSpec((None, ts, w), lambda b, h: (b, 0, h)),
        out_shape=jax.ShapeDtypeStruct((nb, ts, d), BF16),
        compiler_params=_params("parallel", "parallel"),
        name="fox_decode",
    )(q, k_new, v_new, k_cache, v_cache, c_past, c_new)


def _gla_chunk(q, k, v, la2, st, chunk, sub):
    row = lax.broadcasted_iota(jnp.int32, q.shape, 0)
    b = la2
    s = 1
    while s < chunk:
        b = b + jnp.where(row >= s, pltpu.roll(b, s, axis=0), 0.0)
        s *= 2
    b_last = b[chunk - 1:chunk, :]

    o = _dot_nt((q * jnp.exp2(b)).astype(BF16), st.astype(BF16))

    lane = lax.broadcasted_iota(jnp.int32, (sub, chunk), 1)
    srow = lax.broadcasted_iota(jnp.int32, (sub, chunk), 0)
    blocks = []
    for i in range(chunk // sub):
        r0 = i * sub
        qi = q[r0:r0 + sub]
        bi = b[r0:r0 + sub]
        diag = jnp.zeros((sub, chunk), F32)
        for j in range(sub):
            kj = k[r0 + j:r0 + j + 1]
            bj = b[r0 + j:r0 + j + 1]
            col = jnp.sum(qi * kj * jnp.exp2(bi - bj), axis=-1, keepdims=True)
            diag = jnp.where(lane == r0 + j, col, diag)
        a_i = jnp.where(lane - r0 <= srow, diag, 0.0)
        if i > 0:
            ref_b = b[r0:r0 + 1]
            qg = (qi * jnp.exp2(bi - ref_b)).astype(BF16)
            rr = min(chunk, -(-r0 // BF16_ROWS) * BF16_ROWS)
            kg = (k[:rr] * jnp.exp2(jnp.minimum(ref_b - b[:rr], 0.0))).astype(BF16)
            if rr < chunk:
                kg = jnp.concatenate([kg, jnp.zeros((chunk - rr, kg.shape[1]), BF16)], axis=0)
            a_i = jnp.where(lane < r0, _dot_nt(qg, kg), a_i)
        blocks.append(a_i)
    a = jnp.concatenate(blocks, axis=0)
    o = o + _dot(a.astype(BF16), v)

    kd = (k * jnp.exp2(b_last - b)).astype(BF16)
    st_new = st * jnp.exp2(b_last) + _dot_tn(v, kd)
    return o, st_new


def _gla_kernel(q_ref, k_ref, v_ref, la_ref, s0_ref, o_ref, s_out_ref, s_scr, *, chunk, sub, cps, nsteps, scale):
    step = pl.program_id(2)

    @pl.when(step == 0)
    def _():
        s_scr[...] = s0_ref[...]

    st = s_scr[...]
    for ci in range(cps):
        rows = slice(ci * chunk, (ci + 1) * chunk)
        o, st = _gla_chunk(q_ref[rows, :] * scale, k_ref[rows, :], v_ref[rows, :].astype(BF16),
                           la_ref[rows, :] * LOG2E, st, chunk, sub)
        o_ref[rows, :] = o
    s_scr[...] = st

    @pl.when(step == nsteps - 1)
    def _():
        s_out_ref[...] = st


def _gla(qkvr, la, s0t, heads, dk, dv, chunk):
    nb, nt, _ = qkvr.shape
    cps = GLA_CHUNKS_PER_STEP if nt % (GLA_CHUNKS_PER_STEP * chunk) == 0 else 1
    rows = cps * chunk
    nsteps = nt // rows
    kb = heads
    vb = 2 * heads * dk // dv
    assert (2 * heads * dk) % dv == 0
    return pl.pallas_call(
        functools.partial(_gla_kernel, chunk=chunk, sub=min(GLA_SUB, chunk), cps=cps, nsteps=nsteps, scale=dk ** -0.5),
        grid=(nb, heads, nsteps),
        in_specs=[pl.BlockSpec((None, rows, dk), lambda b, h, c: (b, c, h)),
                  pl.BlockSpec((None, rows, dk), lambda b, h, c: (b, c, kb + h)),
                  pl.BlockSpec((None, rows, dv), lambda b, h, c: (b, c, vb + h)),
                  pl.BlockSpec((None, rows, dk), lambda b, h, c: (b, c, h)),
                  pl.BlockSpec((None, None, dv, dk), lambda b, h, c: (b, h, 0, 0))],
        out_specs=[pl.BlockSpec((None, rows, dv), lambda b, h, c: (b, c, h)),
                   pl.BlockSpec((None, None, dv, dk), lambda b, h, c: (b, h, 0, 0))],
        out_shape=[jax.ShapeDtypeStruct((nb, nt, heads * dv), F32),
                   jax.ShapeDtypeStruct((nb, heads, dv, dk), F32)],
        scratch_shapes=[pltpu.VMEM((dv, dk), F32)],
        compiler_params=_params("parallel", "parallel", "arbitrary"),
        name="gla_chunk",
    )(qkvr, qkvr, qkvr, la, s0t)


def _gla_out_kernel(o_ref, r_ref, g_ref, a_ref, *, heads, dv):
    g = g_ref[...]
    for h in range(heads):
        sl = slice(h * dv, (h + 1) * dv)
        x = o_ref[:, sl]
        r = r_ref[:, sl]
        ms = jnp.mean(x * x, axis=-1, keepdims=True)
        a_ref[:, sl] = (x * lax.rsqrt(ms + EPS) * g * (r * jax.nn.sigmoid(r))).astype(a_ref.dtype)


def _gla_out(o, qkvr, g_o, heads, dv):
    m, d = o.shape
    tm = _tile(m, 512)
    rb = qkvr.shape[1] // d - 1
    return pl.pallas_call(
        functools.partial(_gla_out_kernel, heads=heads, dv=dv),
        grid=(m // tm,),
        in_specs=[pl.BlockSpec((tm, d), lambda i: (i, 0)),
                  pl.BlockSpec((tm, d), lambda i: (i, rb)),
                  pl.BlockSpec((1, dv), lambda i: (0, 0))],
        out_specs=pl.BlockSpec((tm, d), lambda i: (i, 0)),
        out_shape=jax.ShapeDtypeStruct((m, d), BF16),
        compiler_params=_params("parallel"),
        name="gla_out_norm",
    )(o, qkvr, g_o.reshape(1, dv))


def _conv_layer(x, nb, nt, hist, g_mix, w_in, w_dw, g_norm, w_out):
    m, d = x.shape
    width = w_dw.shape[0]
    u = _conv_in(x, g_mix, w_in)
    u3 = u.reshape(nb, nt, d)
    if hist is None:
        hist_pad = jnp.zeros((nb, CONV_HALO, d), F32)
        tail = u3[:, nt - (width - 1):]
    else:
        hist_pad = jnp.pad(hist, ((0, 0), (CONV_HALO - (width - 1), 0), (0, 0)))
        tail = jnp.concatenate([hist, u3], axis=1)[:, nt:]
    a = _conv_norm_silu(u3, hist_pad, w_dw, g_norm)
    return _out_proj(a.reshape(m, d), w_out, x), tail


def _pad_cols(w, n):
    return jnp.pad(w, ((0, 0), (0, n - w.shape[1])))


def _fox_layer(xp, xs, dims, cache_k, cache_v, cache_logf, g_mix, w_qkv, w_f, b_f, g_q, g_k, w_o):
    nb, nt, nbs, nts = dims
    d = xp.shape[1]
    heads, hd = cache_k.shape[2], cache_k.shape[3]
    past = cache_k.shape[1]
    hb = FLASH_HEADS_PER_STEP if heads % FLASH_HEADS_PER_STEP == 0 else 1
    w_qkv = w_qkv.astype(BF16)
    w_o = w_o.astype(BF16)
    w_fp = _pad_cols(w_f, LANES).astype(BF16)
    b_fp = jnp.pad(b_f, (0, LANES - heads)).reshape(1, LANES)
    gain_row = jnp.concatenate([jnp.tile(g_q, heads) * (hd ** -0.5 * LOG2E), jnp.tile(g_k, heads),
                                jnp.ones((d,), F32)]).reshape(1, 3 * d)

    def project(x, b_, t_):
        q, k, v, k32, v32, lf = _fox_proj(x, g_mix, w_qkv, gain_row, w_fp, b_fp, heads)
        lf = lf[:, :heads].reshape(b_, t_, heads)
        new = (k32.reshape(b_, t_, heads, hd), v32.reshape(b_, t_, heads, hd), lf)
        return q.reshape(b_, t_, d), k.reshape(b_, t_, d), v.reshape(b_, t_, d), jnp.transpose(lf, (0, 2, 1)), new

    q, k, v, lf_t, new_p = project(xp, nb, nt)
    c = _cumsum_lanes(lf_t)
    o = _flash_attention(q, k, v, c.reshape(nb, heads // hb, hb, nt), heads, hb)
    yp = _out_proj(o.reshape(nb * nt, d), w_o, xp)

    qs, ks, vs, lfs_t, new_s = project(xs, nbs, nts)
    c_past = _cumsum_lanes(jnp.transpose(cache_logf, (0, 2, 1)), sub_last=True)
    nts_pad = -(-nts // LANES) * LANES
    c_new = _cumsum_lanes(jnp.pad(lfs_t, ((0, 0), (0, 0), (0, nts_pad - nts))))[:, :, :nts]
    os_ = _decode_attention(qs, ks, vs, cache_k.reshape(nbs, past, d), cache_v.reshape(nbs, past, d),
                            c_past.reshape(nbs, heads // hb, hb, past), c_new.reshape(nbs, heads // hb, hb, nts),
                            heads, hb)
    ys = _out_proj(os_.reshape(nbs * nts, d), w_o, xs)
    return yp, ys, new_p + new_s


def _gla_group(x, nb, nt, s0t, chunk, heads, dk, dv, g_mix, w_qkvr, w_a1p, w_a2p, b_a, g_o, w_o):
    m, d = x.shape
    qkvr, la = _gla_proj(x, g_mix, w_qkvr, w_a1p, w_a2p, b_a)
    n = qkvr.shape[1]
    o, st = _gla(qkvr.reshape(nb, nt, n), la.reshape(nb, nt, heads * dk), s0t, heads, dk, dv, chunk)
    a = _gla_out(o.reshape(m, d), qkvr, g_o, heads, dv)
    return _out_proj(a, w_o, x), jnp.swapaxes(st, 2, 3)


def _gla_layer(xp, xs, dims, state, g_mix, w_qkvr, w_a1, w_a2, b_a, g_o, w_o):
    nb, nt, nbs, nts = dims
    heads, dk, dv = state.shape[1], state.shape[2], state.shape[3]
    rank = w_a1.shape[1]
    w_qkvr = w_qkvr.astype(BF16)
    w_o = w_o.astype(BF16)
    w_a1p = _pad_cols(w_a1, LANES).astype(BF16)
    w_a2p = jnp.pad(w_a2, ((0, LANES - rank), (0, 0))).astype(BF16)
    common = (heads, dk, dv, g_mix, w_qkvr, w_a1p, w_a2p, b_a, g_o, w_o)
    zeros = jnp.zeros((nb, heads, dv, dk), F32)
    yp, sp = _gla_group(xp, nb, nt, zeros, GLA_CHUNK, *common)
    ys, ss = _gla_group(xs, nbs, nts, jnp.swapaxes(state, 2, 3), nts, *common)
    return yp, ys, (sp, ss)


def kernel(x_prompt, x_sample, cache_conv_l0, cache_k_l1, cache_v_l1, cache_logf_l1, state_gla_l2, cache_conv_l3, norm_mix_l0, conv_w_in_l0, conv_w_dw_l0, conv_norm_l0, conv_w_out_l0, norm_ffn_l0, ffn_w_up_l0, ffn_w_down_l0, norm_mix_l1, fox_w_qkv_l1, fox_w_f_l1, fox_b_f_l1, fox_q_norm_l1, fox_k_norm_l1, fox_w_o_l1, norm_ffn_l1, ffn_w_up_l1, ffn_w_down_l1, norm_mix_l2, gla_w_qkvr_l2, gla_w_a1_l2, gla_w_a2_l2, gla_b_a_l2, gla_o_norm_l2, gla_w_o_l2, norm_ffn_l2, ffn_w_up_l2, ffn_w_down_l2, norm_mix_l3, conv_w_in_l3, conv_w_dw_l3, conv_norm_l3, conv_w_out_l3, norm_ffn_l3, ffn_w_up_l3, ffn_w_down_l3):
    nb, nt, d = x_prompt.shape
    nbs, nts, _ = x_sample.shape
    dims = (nb, nt, nbs, nts)
    xp = x_prompt.reshape(nb * nt, d)
    xs = x_sample.reshape(nbs * nts, d)

    def ffn(xp, xs, g, wu, wd):
        wu = wu.astype(BF16)
        wd = wd.astype(BF16)
        return _ffn(xp, g, wu, wd), _ffn(xs, g, wu, wd)

    def conv(xp, xs, cache, g_mix, w_in, w_dw, g_norm, w_out):
        w_in = w_in.astype(BF16)
        w_out = w_out.astype(BF16)
        yp, sp = _conv_layer(xp, nb, nt, None, g_mix, w_in, w_dw, g_norm, w_out)
        ys, ss = _conv_layer(xs, nbs, nts, cache, g_mix, w_in, w_dw, g_norm, w_out)
        return yp, ys, (sp, ss)

    xp, xs, st0 = conv(xp, xs, cache_conv_l0, norm_mix_l0, conv_w_in_l0, conv_w_dw_l0, conv_norm_l0, conv_w_out_l0)
    xp, xs = ffn(xp, xs, norm_ffn_l0, ffn_w_up_l0, ffn_w_down_l0)
    xp, xs, st1 = _fox_layer(xp, xs, dims, cache_k_l1, cache_v_l1, cache_logf_l1, norm_mix_l1, fox_w_qkv_l1,
                             fox_w_f_l1, fox_b_f_l1, fox_q_norm_l1, fox_k_norm_l1, fox_w_o_l1)
    xp, xs = ffn(xp, xs, norm_ffn_l1, ffn_w_up_l1, ffn_w_down_l1)
    xp, xs, st2 = _gla_layer(xp, xs, dims, state_gla_l2, norm_mix_l2, gla_w_qkvr_l2, gla_w_a1_l2, gla_w_a2_l2,
                             gla_b_a_l2, gla_o_norm_l2, gla_w_o_l2)
    xp, xs = ffn(xp, xs, norm_ffn_l2, ffn_w_up_l2, ffn_w_down_l2)
    xp, xs, st3 = conv(xp, xs, cache_conv_l3, norm_mix_l3, conv_w_in_l3, conv_w_dw_l3, conv_norm_l3, conv_w_out_l3)
    xp, xs = ffn(xp, xs, norm_ffn_l3, ffn_w_up_l3, ffn_w_down_l3)
    return (xp.reshape(nb, nt, d), xs.reshape(nbs, nts, d), *st0, *st1, *st2, *st3)
```

```python
import functools
import math

import jax
import jax.numpy as jnp
from jax import lax
from jax.experimental import pallas as pl
from jax.experimental.pallas import tpu as pltpu

F32 = jnp.float32
BF16 = jnp.bfloat16

EPS = 1e-6
LOG2E = math.log2(math.e)
GLA_CHUNK = 64
GLA_SUB = 8
GLA_CHUNKS_PER_STEP = 8
GLA_TAU = 16.0
FLASH_ROW_CHUNK = 32
FLASH_HEADS_PER_STEP = 8
LANES = 128
SUBLANES = 8
BF16_ROWS = 2 * SUBLANES
CONV_HALO = 32
VMEM_LIMIT_BYTES = 56 * 1024 * 1024


def _tile(n, pref):
    if n <= pref:
        return n
    t = pref - pref % 8
    while t >= 8:
        if n % t == 0:
            return t
        t -= 8
    return n


def _params(*sem):
    return pltpu.CompilerParams(dimension_semantics=sem, vmem_limit_bytes=VMEM_LIMIT_BYTES)


def _log_sigmoid(x):
    return jnp.minimum(x, 0.0) - jnp.log(1.0 + jnp.exp(-jnp.abs(x)))


def _rms_bf16(x, g):
    ms = jnp.mean(x * x, axis=-1, keepdims=True)
    return (x * lax.rsqrt(ms + EPS) * g).astype(BF16)


def _dot(a, b):
    return jnp.dot(a, b, preferred_element_type=F32)


def _dot_nt(a, b):
    return lax.dot_general(a, b, (((1,), (1,)), ((), ())), preferred_element_type=F32)


def _dot_tn(a, b):
    return lax.dot_general(a, b, (((0,), (0,)), ((), ())), preferred_element_type=F32)


def _conv_in_kernel(x_ref, g_ref, wa_ref, wb_ref, o_ref, h_scr):
    @pl.when(pl.program_id(1) == 0)
    def _():
        h_scr[...] = _rms_bf16(x_ref[...], g_ref[...])

    h = h_scr[...]
    o_ref[...] = _dot(h, wa_ref[...]) * jax.nn.sigmoid(_dot(h, wb_ref[...]))


def _conv_in(x, g, w_in):
    m, d = x.shape
    n = w_in.shape[1] // 2
    tm = _tile(m, 1024)
    tn = _tile(n, 512)
    nb = n // tn
    return pl.pallas_call(
        _conv_in_kernel,
        grid=(m // tm, nb),
        in_specs=[pl.BlockSpec((tm, d), lambda i, j: (i, 0)),
                  pl.BlockSpec((1, d), lambda i, j: (0, 0)),
                  pl.BlockSpec((d, tn), lambda i, j: (0, j)),
                  pl.BlockSpec((d, tn), lambda i, j: (0, j + nb))],
        out_specs=pl.BlockSpec((tm, tn), lambda i, j: (i, j)),
        out_shape=jax.ShapeDtypeStruct((m, n), F32),
        scratch_shapes=[pltpu.VMEM((tm, d), BF16)],
        compiler_params=_params("parallel", "arbitrary"),
        name="conv_in_glu",
    )(x, g.reshape(1, d), w_in, w_in)


def _fox_proj_kernel(x_ref, g_ref, w_ref, gain_ref, wf_ref, bf_ref, q_ref, k_ref, v_ref, k32_ref, v32_ref, lf_ref,
                     h_scr, *, nq, heads_per_tile, head_dim):
    j = pl.program_id(1)

    @pl.when(j == 0)
    def _():
        h = _rms_bf16(x_ref[...], g_ref[...])
        h_scr[...] = h
        lf_ref[...] = _log_sigmoid(_dot(h, wf_ref[...]) + bf_ref[...])

    acc = _dot(h_scr[...], w_ref[...])

    def head_norm(h):
        sl = slice(h * head_dim, (h + 1) * head_dim)
        x = acc[:, sl]
        ms = jnp.mean(x * x, axis=-1, keepdims=True)
        return sl, x * lax.rsqrt(ms + EPS) * gain_ref[:, sl]

    @pl.when(j < nq)
    def _():
        for h in range(heads_per_tile):
            sl, y = head_norm(h)
            q_ref[:, sl] = y.astype(q_ref.dtype)

    @pl.when((j >= nq) & (j < 2 * nq))
    def _():
        for h in range(heads_per_tile):
            sl, y = head_norm(h)
            k_ref[:, sl] = y.astype(k_ref.dtype)
            k32_ref[:, sl] = y

    @pl.when(j >= 2 * nq)
    def _():
        v_ref[...] = acc.astype(v_ref.dtype)
        v32_ref[...] = acc


def _fox_proj(x, g, w_qkv, gain_row, w_fp, b_fp, heads):
    m, d = x.shape
    n = w_qkv.shape[1]
    head_dim = d // heads
    tm = _tile(m, 1024)
    tn = _tile(d, 512)
    nq = d // tn
    assert tn % head_dim == 0 and n == 3 * d

    def cols(first):
        return lambda i, j: (i, jnp.clip(j - first, 0, nq - 1))

    return pl.pallas_call(
        functools.partial(_fox_proj_kernel, nq=nq, heads_per_tile=tn // head_dim, head_dim=head_dim),
        grid=(m // tm, n // tn),
        in_specs=[pl.BlockSpec((tm, d), lambda i, j: (i, 0)),
                  pl.BlockSpec((1, d), lambda i, j: (0, 0)),
                  pl.BlockSpec((d, tn), lambda i, j: (0, j)),
                  pl.BlockSpec((1, tn), lambda i, j: (0, j)),
                  pl.BlockSpec((d, LANES), lambda i, j: (0, 0)),
                  pl.BlockSpec((1, LANES), lambda i, j: (0, 0))],
        out_specs=[pl.BlockSpec((tm, tn), cols(0)),
                   pl.BlockSpec((tm, tn), cols(nq)),
                   pl.BlockSpec((tm, tn), cols(2 * nq)),
                   pl.BlockSpec((tm, tn), cols(nq)),
                   pl.BlockSpec((tm, tn), cols(2 * nq)),
                   pl.BlockSpec((tm, LANES), lambda i, j: (i, 0))],
        out_shape=[jax.ShapeDtypeStruct((m, d), BF16), jax.ShapeDtypeStruct((m, d), BF16),
                   jax.ShapeDtypeStruct((m, d), BF16), jax.ShapeDtypeStruct((m, d), F32),
                   jax.ShapeDtypeStruct((m, d), F32), jax.ShapeDtypeStruct((m, LANES), F32)],
        scratch_shapes=[pltpu.VMEM((tm, d), BF16)],
        compiler_params=_params("parallel", "arbitrary"),
        name="fox_proj",
    )(x, g.reshape(1, d), w_qkv, gain_row, w_fp, b_fp)


def _gla_proj_kernel(x_ref, g_ref, w_ref, w1_ref, w2_ref, ba_ref, o_ref, la_ref, h_scr):
    @pl.when(pl.program_id(1) == 0)
    def _():
        h = _rms_bf16(x_ref[...], g_ref[...])
        h_scr[...] = h
        low = _dot(h, w1_ref[...]).astype(BF16)
        la_ref[...] = _log_sigmoid(_dot(low, w2_ref[...]) + ba_ref[...]) * (1.0 / GLA_TAU)

    o_ref[...] = _dot(h_scr[...], w_ref[...])


def _gla_proj(x, g, w_qkvr, w_a1p, w_a2p, b_a):
    m, d = x.shape
    n = w_qkvr.shape[1]
    na = w_a2p.shape[1]
    tm = _tile(m, 1024)
    tn = _tile(n, 1024)
    return pl.pallas_call(
        _gla_proj_kernel,
        grid=(m // tm, n // tn),
        in_specs=[pl.BlockSpec((tm, d), lambda i, j: (i, 0)),
                  pl.BlockSpec((1, d), lambda i, j: (0, 0)),
                  pl.BlockSpec((d, tn), lambda i, j: (0, j)),
                  pl.BlockSpec(w_a1p.shape, lambda i, j: (0, 0)),
                  pl.BlockSpec(w_a2p.shape, lambda i, j: (0, 0)),
                  pl.BlockSpec((1, na), lambda i, j: (0, 0))],
        out_specs=[pl.BlockSpec((tm, tn), lambda i, j: (i, j)),
                   pl.BlockSpec((tm, na), lambda i, j: (i, 0))],
        out_shape=[jax.ShapeDtypeStruct((m, n), F32), jax.ShapeDtypeStruct((m, na), F32)],
        scratch_shapes=[pltpu.VMEM((tm, d), BF16)],
        compiler_params=_params("parallel", "arbitrary"),
        name="gla_proj",
    )(x, g.reshape(1, d), w_qkvr, w_a1p, w_a2p, b_a.reshape(1, na))


def _out_proj_kernel(a_ref, w_ref, r_ref, o_ref, *, nc):
    a = a_ref[...]
    cw = o_ref.shape[1] // nc
    for c in range(nc):
        cs = slice(c * cw, (c + 1) * cw)
        o_ref[:, cs] = r_ref[:, cs] + _dot(a, w_ref[:, cs])


def _out_proj(a, w, res):
    m, k = a.shape
    n = w.shape[1]
    tm = _tile(m, 512)
    nc = max(1, n // 512)
    return pl.pallas_call(
        functools.partial(_out_proj_kernel, nc=nc),
        grid=(m // tm,),
        in_specs=[pl.BlockSpec((tm, k), lambda i: (i, 0)),
                  pl.BlockSpec((k, n), lambda i: (0, 0)),
                  pl.BlockSpec((tm, n), lambda i: (i, 0))],
        out_specs=pl.BlockSpec((tm, n), lambda i: (i, 0)),
        out_shape=jax.ShapeDtypeStruct((m, n), F32),
        compiler_params=_params("parallel"),
        name="out_proj",
    )(a, w, res)


def _ffn_kernel(x_ref, g_ref, wu_ref, wd_ref, o_ref, h_scr):
    @pl.when(pl.program_id(1) == 0)
    def _():
        x = x_ref[...]
        h_scr[...] = _rms_bf16(x, g_ref[...])
        o_ref[...] = x

    u = jnp.maximum(_dot(h_scr[...], wu_ref[...]), 0.0)
    u = (u * u).astype(BF16)
    d = o_ref.shape[1]
    dc = min(d, 512)
    for c in range(d // dc):
        cs = slice(c * dc, (c + 1) * dc)
        o_ref[:, cs] += _dot(u, wd_ref[:, cs])


def _ffn(x, g, wu, wd):
    m, d = x.shape
    f = wu.shape[1]
    tm = _tile(m, 1024)
    tf = _tile(f, 512 if tm > 256 else 2048)
    return pl.pallas_call(
        _ffn_kernel,
        grid=(m // tm, f // tf),
        in_specs=[pl.BlockSpec((tm, d), lambda i, j: (i, 0)),
                  pl.BlockSpec((1, d), lambda i, j: (0, 0)),
                  pl.BlockSpec((d, tf), lambda i, j: (0, j)),
                  pl.BlockSpec((tf, d), lambda i, j: (j, 0))],
        out_specs=pl.BlockSpec((tm, d), lambda i, j: (i, 0)),
        out_shape=jax.ShapeDtypeStruct((m, d), F32),
        scratch_shapes=[pltpu.VMEM((tm, d), BF16)],
        compiler_params=_params("parallel", "arbitrary"),
        name="ffn",
    )(x, g.reshape(1, d), wu, wd)


def _shift_rows_up(x, r):
    n, c = x.shape
    x3 = x.reshape(n // SUBLANES, SUBLANES, c)
    rot = pltpu.roll(x3, SUBLANES - r, axis=1)
    sub = lax.broadcasted_iota(jnp.int32, (n // SUBLANES - 1, SUBLANES, c), 1)
    return jnp.where(sub < SUBLANES - r, rot[:-1], rot[1:]).reshape(n - SUBLANES, c)


def _conv_kernel(u_ref, up_ref, hist_ref, w_ref, g_ref, o_ref, ext_scr, y_scr, *, tm, width, rb, lc):
    i = pl.program_id(1)
    d = u_ref.shape[-1]
    nr = BF16_ROWS

    @pl.when(i == 0)
    def _():
        ext_scr[0:CONV_HALO, :] = hist_ref[...]

    @pl.when(i > 0)
    def _():
        ext_scr[0:CONV_HALO, :] = up_ref[...]

    ext_scr[CONV_HALO:CONV_HALO + tm, :] = u_ref[...]
    off = CONV_HALO - (width - 1)

    def row_block(r, carry):
        r0 = pl.multiple_of(r * rb, rb)
        for c in range(d // lc):
            cs = slice(c * lc, (c + 1) * lc)
            acc = jnp.zeros((rb, lc), F32)
            for rem in range(SUBLANES):
                part = None
                for k in range(width):
                    if (off + k) % SUBLANES != rem:
                        continue
                    base = (off + k) // SUBLANES * SUBLANES
                    rows = rb + SUBLANES if rem else rb
                    win = ext_scr[pl.ds(r0 + base, rows), cs].reshape(rows // SUBLANES, SUBLANES, lc)
                    term = (win * w_ref[k * SUBLANES:(k + 1) * SUBLANES, cs][None]).reshape(rows, lc)
                    part = term if part is None else part + term
                if part is None:
                    continue
                acc = acc + (_shift_rows_up(part, rem) if rem else part)
            y_scr[pl.ds(r0, rb), cs] = acc
        for s in range(rb // nr):
            rows = pl.ds(r0 + s * nr, nr)
            y = y_scr[rows, :]
            ms = jnp.mean(y * y, axis=-1, keepdims=True)
            z = y * lax.rsqrt(ms + EPS) * g_ref[...]
            o_ref[rows, :] = (z * jax.nn.sigmoid(z)).astype(o_ref.dtype)
        return carry

    lax.fori_loop(0, tm // rb, row_block, 0)


def _conv_norm_silu(u, hist_pad, w_dw, g):
    nb, nt, d = u.shape
    width = w_dw.shape[0]
    assert width - 1 <= CONV_HALO
    tm = _tile(nt, 512)
    assert tm % CONV_HALO == 0
    rb = 64 if tm % 64 == 0 else CONV_HALO
    lc = min(256, d)
    hb = tm // CONV_HALO
    w_rep = jnp.repeat(w_dw, SUBLANES, axis=0)
    return pl.pallas_call(
        functools.partial(_conv_kernel, tm=tm, width=width, rb=rb, lc=lc),
        grid=(nb, nt // tm),
        in_specs=[pl.BlockSpec((None, tm, d), lambda b, i: (b, i, 0)),
                  pl.BlockSpec((None, CONV_HALO, d), lambda b, i: (b, jnp.maximum(i * hb - 1, 0), 0)),
                  pl.BlockSpec((None, CONV_HALO, d), lambda b, i: (b, 0, 0)),
                  pl.BlockSpec((width * SUBLANES, d), lambda b, i: (0, 0)),
                  pl.BlockSpec((1, d), lambda b, i: (0, 0))],
        out_specs=pl.BlockSpec((None, tm, d), lambda b, i: (b, i, 0)),
        out_shape=jax.ShapeDtypeStruct((nb, nt, d), BF16),
        scratch_shapes=[pltpu.VMEM((CONV_HALO + tm, d), F32), pltpu.VMEM((tm, d), F32)],
        compiler_params=_params("parallel", "arbitrary"),
        name="conv_norm_silu",
    )(u, u, hist_pad, w_rep, g.reshape(1, d))


def _cumsum_kernel(x_ref, o_ref, *, n, sub_last, scale):
    x = x_ref[...]
    lane = lax.broadcasted_iota(jnp.int32, x.shape, 1)
    s = 1
    while s < n:
        x = x + jnp.where(lane >= s, pltpu.roll(x, s, axis=1), 0.0)
        s *= 2
    if sub_last:
        x = x - x[:, n - 1:n]
    o_ref[...] = x * scale


def _cumsum_lanes(x, *, sub_last=False, scale=LOG2E):
    nb, h, n = x.shape
    return pl.pallas_call(
        functools.partial(_cumsum_kernel, n=n, sub_last=sub_last, scale=scale),
        grid=(nb,),
        in_specs=[pl.BlockSpec((None, h, n), lambda b: (b, 0, 0))],
        out_specs=pl.BlockSpec((None, h, n), lambda b: (b, 0, 0)),
        out_shape=jax.ShapeDtypeStruct((nb, h, n), F32),
        compiler_params=_params("parallel"),
        name="cumsum_logf",
    )(x)


def _flash_kernel(q_ref, k_ref, v_ref, c_ref, o_ref, s_scr, p_scr, m_scr, l_scr, acc_scr, *, tq, hd, hb, rc):
    qi = pl.program_id(2)
    heads = [slice(h * hd, (h + 1) * hd) for h in range(hb)]
    qs = [q_ref[:, hs] for hs in heads]
    m_scr[...] = jnp.full(m_scr.shape, -jnp.inf, F32)
    l_scr[...] = jnp.zeros(l_scr.shape, F32)
    acc_scr[...] = jnp.zeros(acc_scr.shape, F32)

    def step(kj, masked):
        k0 = pl.multiple_of(kj * tq, tq)
        for h, hs in enumerate(heads):
            s_scr[h] = _dot_nt(qs[h], k_ref[pl.ds(k0, tq), hs])
            c_row = c_ref[h:h + 1, pl.ds(k0, tq)]
            m_all = m_scr[h]
            l_all = l_scr[h]
            m_out, l_out, a_out = [], [], []
            for r in range(tq // rc):
                rows = slice(r * rc, (r + 1) * rc)
                nvis = min(tq, (r + 1) * rc + LANES - 1) // LANES if masked else tq // LANES
                blocks = []
                for i in range(nvis):
                    cols = slice(i * LANES, (i + 1) * LANES)
                    blk = s_scr[h, rows, cols] - c_row[:, cols]
                    if masked and (i + 1) * LANES - 1 > r * rc:
                        row = lax.broadcasted_iota(jnp.int32, (rc, LANES), 0) + r * rc
                        col = lax.broadcasted_iota(jnp.int32, (rc, LANES), 1) + i * LANES
                        blk = jnp.where(col <= row, blk, -jnp.inf)
                    blocks.append(blk)
                m_old = m_all[rows]
                m_new = jnp.maximum(m_old, jnp.max(functools.reduce(jnp.maximum, blocks), axis=-1, keepdims=True))
                alpha = jnp.exp2(m_old - m_new)
                ps = [jnp.exp2(blk - m_new) for blk in blocks]
                l_out.append(alpha * l_all[rows] + jnp.sum(functools.reduce(jnp.add, ps), axis=-1, keepdims=True))
                m_out.append(m_new)
                a_out.append(alpha)
                for i in range(tq // LANES):
                    pb = ps[i].astype(BF16) if i < nvis else jnp.zeros((rc, LANES), BF16)
                    p_scr[h, rows, i * LANES:(i + 1) * LANES] = pb
            m_scr[h] = jnp.concatenate(m_out, axis=0)
            l_scr[h] = jnp.concatenate(l_out, axis=0)
            alpha = jnp.concatenate(a_out, axis=0)
            acc_scr[h] = alpha * acc_scr[h] + _dot(p_scr[h], v_ref[pl.ds(k0, tq), hs])

    def body(kj, carry):
        step(kj, False)
        return carry

    lax.fori_loop(0, qi, body, 0)
    step(qi, True)
    for h, hs in enumerate(heads):
        o_ref[:, hs] = (acc_scr[h] / l_scr[h]).astype(o_ref.dtype)


def _flash_attention(q, k, v, c, heads, hb):
    nb, nt, d = q.shape
    hd = d // heads
    assert hd == LANES
    tq = _tile(nt, 512)
    w = hb * hd
    nhb = heads // hb
    return pl.pallas_call(
        functools.partial(_flash_kernel, tq=tq, hd=hd, hb=hb, rc=min(FLASH_ROW_CHUNK, tq)),
        grid=(nb, nhb, nt // tq),
        in_specs=[pl.BlockSpec((None, tq, w), lambda b, h, i: (b, i, h)),
                  pl.BlockSpec((None, nt, w), lambda b, h, i: (b, 0, h)),
                  pl.BlockSpec((None, nt, w), lambda b, h, i: (b, 0, h)),
                  pl.BlockSpec((None, None, hb, nt), lambda b, h, i: (b, h, 0, 0))],
        out_specs=pl.BlockSpec((None, tq, w), lambda b, h, i: (b, i, h)),
        out_shape=jax.ShapeDtypeStruct((nb, nt, d), BF16),
        scratch_shapes=[pltpu.VMEM((hb, tq, tq), F32), pltpu.VMEM((hb, tq, tq), BF16),
                        pltpu.VMEM((hb, tq, LANES), F32), pltpu.VMEM((hb, tq, LANES), F32),
                        pltpu.VMEM((hb, tq, hd), F32)],
        compiler_params=_params("parallel", "parallel", "arbitrary"),
        name="fox_flash",
    )(q, k, v, c)


def _decode_kernel(q_ref, kn_ref, vn_ref, kc_ref, vc_ref, cp_ref, cn_ref, o_ref, *, hd, hb):
    ts = q_ref.shape[0]
    row = lax.broadcasted_iota(jnp.int32, (ts, ts), 0)
    col = lax.broadcasted_iota(jnp.int32, (ts, ts), 1)
    for h in range(hb):
        hs = slice(h * hd, (h + 1) * hd)
        q = q_ref[:, hs]
        s_p = _dot_nt(q, kc_ref[:, hs].astype(BF16)) - cp_ref[h:h + 1, :]
        s_n = _dot_nt(q, kn_ref[:, hs]) - cn_ref[h:h + 1, :]
        s_n = jnp.where(col <= row, s_n, -jnp.inf)
        m = jnp.maximum(jnp.max(s_p, axis=-1, keepdims=True), jnp.max(s_n, axis=-1, keepdims=True))
        p_p = jnp.exp2(s_p - m)
        p_n = jnp.exp2(s_n - m)
        l = jnp.sum(p_p, axis=-1, keepdims=True) + jnp.sum(p_n, axis=-1, keepdims=True)
        acc = (_dot(p_p.astype(BF16), vc_ref[:, hs].astype(BF16))
               + _dot(p_n.astype(BF16), vn_ref[:, hs]))
        o_ref[:, hs] = (acc / l).astype(o_ref.dtype)


def _decode_attention(q, k_new, v_new, k_cache, v_cache, c_past, c_new, heads, hb):
    nb, ts, d = q.shape
    p = k_cache.shape[1]
    hd = d // heads
    w = hb * hd
    return pl.pallas_call(
        functools.partial(_decode_kernel, hd=hd, hb=hb),
        grid=(nb, heads // hb),
        in_specs=[pl.BlockSpec((None, ts, w), lambda b, h: (b, 0, h)),
                  pl.BlockSpec((None, ts, w), lambda b, h: (b, 0, h)),
                  pl.BlockSpec((None, ts, w), lambda b, h: (b, 0, h)),
                  pl.BlockSpec((None, p, w), lambda b, h: (b, 0, h)),
                  pl.BlockSpec((None, p, w), lambda b, h: (b, 0, h)),
                  pl.BlockSpec((None, None, hb, p), lambda b, h: (b, h, 0, 0)),
                  pl.BlockSpec((None, None, hb, ts), lambda b, h: (b, h, 0, 0))],
        out_specs=pl.BlockSpec((None, ts, w), lambda b, h: (b, 0, h)),
        out_shape=jax.ShapeDtypeStruct((nb, ts, d), BF16),
        compiler_params=_params("parallel", "parallel"),
        name="fox_decode",
    )(q, k_new, v_new, k_cache, v_cache, c_past, c_new)


def _gla_chunk(q, k, v, la2, st, chunk, sub):
    row = lax.broadcasted_iota(jnp.int32, q.shape, 0)
    b = la2
    s = 1
    while s < chunk:
        b = b + jnp.where(row >= s, pltpu.roll(b, s, axis=0), 0.0)
        s *= 2
    b_last = b[chunk - 1:chunk, :]

    o = _dot_nt((q * jnp.exp2(b)).astype(BF16), st.astype(BF16))

    lane = lax.broadcasted_iota(jnp.int32, (sub, chunk), 1)
    srow = lax.broadcasted_iota(jnp.int32, (sub, chunk), 0)
    blocks = []
    for i in range(chunk // sub):
        r0 = i * sub
        qi = q[r0:r0 + sub]
        bi = b[r0:r0 + sub]
        diag = jnp.zeros((sub, chunk), F32)
        for j in range(sub):
            kj = k[r0 + j:r0 + j + 1]
            bj = b[r0 + j:r0 + j + 1]
            col = jnp.sum(qi * kj * jnp.exp2(bi - bj), axis=-1, keepdims=True)
            diag = jnp.where(lane == r0 + j, col, diag)
        a_i = jnp.where(lane - r0 <= srow, diag, 0.0)
        if i > 0:
            ref_b = b[r0:r0 + 1]
            qg = (qi * jnp.exp2(bi - ref_b)).astype(BF16)
            rr = min(chunk, -(-r0 // BF16_ROWS) * BF16_ROWS)
            kg = (k[:rr] * jnp.exp2(jnp.minimum(ref_b - b[:rr], 0.0))).astype(BF16)
            if rr < chunk:
                kg = jnp.concatenate([kg, jnp.zeros((chunk - rr, kg.shape[1]), BF16)], axis=0)
            a_i = jnp.where(lane < r0, _dot_nt(qg, kg), a_i)
        blocks.append(a_i)
    a = jnp.concatenate(blocks, axis=0)
    o = o + _dot(a.astype(BF16), v)

    kd = (k * jnp.exp2(b_last - b)).astype(BF16)
    st_new = st * jnp.exp2(b_last) + _dot_tn(v, kd)
    return o, st_new


def _gla_kernel(q_ref, k_ref, v_ref, la_ref, s0_ref, o_ref, s_out_ref, s_scr, *, chunk, sub, cps, nsteps, scale):
    step = pl.program_id(2)

    @pl.when(step == 0)
    def _():
        s_scr[...] = s0_ref[...]

    st = s_scr[...]
    for ci in range(cps):
        rows = slice(ci * chunk, (ci + 1) * chunk)
        o, st = _gla_chunk(q_ref[rows, :] * scale, k_ref[rows, :], v_ref[rows, :].astype(BF16),
                           la_ref[rows, :] * LOG2E, st, chunk, sub)
        o_ref[rows, :] = o
    s_scr[...] = st

    @pl.when(step == nsteps - 1)
    def _():
        s_out_ref[...] = st


def _gla(qkvr, la, s0t, heads, dk, dv, chunk):
    nb, nt, _ = qkvr.shape
    cps = GLA_CHUNKS_PER_STEP if nt % (GLA_CHUNKS_PER_STEP * chunk) == 0 else 1
    rows = cps * chunk
    nsteps = nt // rows
    kb = heads
    vb = 2 * heads * dk // dv
    assert (2 * heads * dk) % dv == 0
    return pl.pallas_call(
        functools.partial(_gla_kernel, chunk=chunk, sub=min(GLA_SUB, chunk), cps=cps, nsteps=nsteps, scale=dk ** -0.5),
        grid=(nb, heads, nsteps),
        in_specs=[pl.BlockSpec((None, rows, dk), lambda b, h, c: (b, c, h)),
                  pl.BlockSpec((None, rows, dk), lambda b, h, c: (b, c, kb + h)),
                  pl.BlockSpec((None, rows, dv), lambda b, h, c: (b, c, vb + h)),
                  pl.BlockSpec((None, rows, dk), lambda b, h, c: (b, c, h)),
                  pl.BlockSpec((None, None, dv, dk), lambda b, h, c: (b, h, 0, 0))],
        out_specs=[pl.BlockSpec((None, rows, dv), lambda b, h, c: (b, c, h)),
                   pl.BlockSpec((None, None, dv, dk), lambda b, h, c: (b, h, 0, 0))],
        out_shape=[jax.ShapeDtypeStruct((nb, nt, heads * dv), F32),
                   jax.ShapeDtypeStruct((nb, heads, dv, dk), F32)],
        scratch_shapes=[pltpu.VMEM((dv, dk), F32)],
        compiler_params=_params("parallel", "parallel", "arbitrary"),
        name="gla_chunk",
    )(qkvr, qkvr, qkvr, la, s0t)


def _gla_out_kernel(o_ref, r_ref, g_ref, a_ref, *, heads, dv):
    g = g_ref[...]
    for h in range(heads):
        sl = slice(h * dv, (h + 1) * dv)
        x = o_ref[:, sl]
        r = r_ref[:, sl]
        ms = jnp.mean(x * x, axis=-1, keepdims=True)
        a_ref[:, sl] = (x * lax.rsqrt(ms + EPS) * g * (r * jax.nn.sigmoid(r))).astype(a_ref.dtype)


def _gla_out(o, qkvr, g_o, heads, dv):
    m, d = o.shape
    tm = _tile(m, 512)
    rb = qkvr.shape[1] // d - 1
    return pl.pallas_call(
        functools.partial(_gla_out_kernel, heads=heads, dv=dv),
        grid=(m // tm,),
        in_specs=[pl.BlockSpec((tm, d), lambda i: (i, 0)),
                  pl.BlockSpec((tm, d), lambda i: (i, rb)),
                  pl.BlockSpec((1, dv), lambda i: (0, 0))],
        out_specs=pl.BlockSpec((tm, d), lambda i: (i, 0)),
        out_shape=jax.ShapeDtypeStruct((m, d), BF16),
        compiler_params=_params("parallel"),
        name="gla_out_norm",
    )(o, qkvr, g_o.reshape(1, dv))


def _conv_layer(x, nb, nt, hist, g_mix, w_in, w_dw, g_norm, w_out):
    m, d = x.shape
    width = w_dw.shape[0]
    u = _conv_in(x, g_mix, w_in)
    u3 = u.reshape(nb, nt, d)
    if hist is None:
        hist_pad = jnp.zeros((nb, CONV_HALO, d), F32)
        tail = u3[:, nt - (width - 1):]
    else:
        hist_pad = jnp.pad(hist, ((0, 0), (CONV_HALO - (width - 1), 0), (0, 0)))
        tail = jnp.concatenate([hist, u3], axis=1)[:, nt:]
    a = _conv_norm_silu(u3, hist_pad, w_dw, g_norm)
    return _out_proj(a.reshape(m, d), w_out, x), tail


def _pad_cols(w, n):
    return jnp.pad(w, ((0, 0), (0, n - w.shape[1])))


def _fox_layer(xp, xs, dims, cache_k, cache_v, cache_logf, g_mix, w_qkv, w_f, b_f, g_q, g_k, w_o):
    nb, nt, nbs, nts = dims
    d = xp.shape[1]
    heads, hd = cache_k.shape[2], cache_k.shape[3]
    past = cache_k.shape[1]
    hb = FLASH_HEADS_PER_STEP if heads % FLASH_HEADS_PER_STEP == 0 else 1
    w_qkv = w_qkv.astype(BF16)
    w_o = w_o.astype(BF16)
    w_fp = _pad_cols(w_f, LANES).astype(BF16)
    b_fp = jnp.pad(b_f, (0, LANES - heads)).reshape(1, LANES)
    gain_row = jnp.concatenate([jnp.tile(g_q, heads) * (hd ** -0.5 * LOG2E), jnp.tile(g_k, heads),
                                jnp.ones((d,), F32)]).reshape(1, 3 * d)

    def project(x, b_, t_):
        q, k, v, k32, v32, lf = _fox_proj(x, g_mix, w_qkv, gain_row, w_fp, b_fp, heads)
        lf = lf[:, :heads].reshape(b_, t_, heads)
        new = (k32.reshape(b_, t_, heads, hd), v32.reshape(b_, t_, heads, hd), lf)
        return q.reshape(b_, t_, d), k.reshape(b_, t_, d), v.reshape(b_, t_, d), jnp.transpose(lf, (0, 2, 1)), new

    q, k, v, lf_t, new_p = project(xp, nb, nt)
    c = _cumsum_lanes(lf_t)
    o = _flash_attention(q, k, v, c.reshape(nb, heads // hb, hb, nt), heads, hb)
    yp = _out_proj(o.reshape(nb * nt, d), w_o, xp)

    qs, ks, vs, lfs_t, new_s = project(xs, nbs, nts)
    c_past = _cumsum_lanes(jnp.transpose(cache_logf, (0, 2, 1)), sub_last=True)
    nts_pad = -(-nts // LANES) * LANES
    c_new = _cumsum_lanes(jnp.pad(lfs_t, ((0, 0), (0, 0), (0, nts_pad - nts))))[:, :, :nts]
    os_ = _decode_attention(qs, ks, vs, cache_k.reshape(nbs, past, d), cache_v.reshape(nbs, past, d),
                            c_past.reshape(nbs, heads // hb, hb, past), c_new.reshape(nbs, heads // hb, hb, nts),
                            heads, hb)
    ys = _out_proj(os_.reshape(nbs * nts, d), w_o, xs)
    return yp, ys, new_p + new_s


def _gla_group(x, nb, nt, s0t, chunk, heads, dk, dv, g_mix, w_qkvr, w_a1p, w_a2p, b_a, g_o, w_o):
    m, d = x.shape
    qkvr, la = _gla_proj(x, g_mix, w_qkvr, w_a1p, w_a2p, b_a)
    n = qkvr.shape[1]
    o, st = _gla(qkvr.reshape(nb, nt, n), la.reshape(nb, nt, heads * dk), s0t, heads, dk, dv, chunk)
    a = _gla_out(o.reshape(m, d), qkvr, g_o, heads, dv)
    return _out_proj(a, w_o, x), jnp.swapaxes(st, 2, 3)


def _gla_layer(xp, xs, dims, state, g_mix, w_qkvr, w_a1, w_a2, b_a, g_o, w_o):
    nb, nt, nbs, nts = dims
    heads, dk, dv = state.shape[1], state.shape[2], state.shape[3]
    rank = w_a1.shape[1]
    w_qkvr = w_qkvr.astype(BF16)
    w_o = w_o.astype(BF16)
    w_a1p = _pad_cols(w_a1, LANES).astype(BF16)
    w_a2p = jnp.pad(w_a2, ((0, LANES - rank), (0, 0))).astype(BF16)
    common = (heads, dk, dv, g_mix, w_qkvr, w_a1p, w_a2p, b_a, g_o, w_o)
    zeros = jnp.zeros((nb, heads, dv, dk), F32)
    yp, sp = _gla_group(xp, nb, nt, zeros, GLA_CHUNK, *common)
    ys, ss = _gla_group(xs, nbs, nts, jnp.swapaxes(state, 2, 3), nts, *common)
    return yp, ys, (sp, ss)


def kernel(x_prompt, x_sample, cache_conv_l0, cache_k_l1, cache_v_l1, cache_logf_l1, state_gla_l2, cache_conv_l3, norm_mix_l0, conv_w_in_l0, conv_w_dw_l0, conv_norm_l0, conv_w_out_l0, norm_ffn_l0, ffn_w_up_l0, ffn_w_down_l0, norm_mix_l1, fox_w_qkv_l1, fox_w_f_l1, fox_b_f_l1, fox_q_norm_l1, fox_k_norm_l1, fox_w_o_l1, norm_ffn_l1, ffn_w_up_l1, ffn_w_down_l1, norm_mix_l2, gla_w_qkvr_l2, gla_w_a1_l2, gla_w_a2_l2, gla_b_a_l2, gla_o_norm_l2, gla_w_o_l2, norm_ffn_l2, ffn_w_up_l2, ffn_w_down_l2, norm_mix_l3, conv_w_in_l3, conv_w_dw_l3, conv_norm_l3, conv_w_out_l3, norm_ffn_l3, ffn_w_up_l3, ffn_w_down_l3):
    nb, nt, d = x_prompt.shape
    nbs, nts, _ = x_sample.shape
    dims = (nb, nt, nbs, nts)
    xp = x_prompt.reshape(nb * nt, d)
    xs = x_sample.reshape(nbs * nts, d)

    def ffn(xp, xs, g, wu, wd):
        wu = wu.astype(BF16)
        wd = wd.astype(BF16)
        return _ffn(xp, g, wu, wd), _ffn(xs, g, wu, wd)

    def conv(xp, xs, cache, g_mix, w_in, w_dw, g_norm, w_out):
        w_in = w_in.astype(BF16)
        w_out = w_out.astype(BF16)
        yp, sp = _conv_layer(xp, nb, nt, None, g_mix, w_in, w_dw, g_norm, w_out)
        ys, ss = _conv_layer(xs, nbs, nts, cache, g_mix, w_in, w_dw, g_norm, w_out)
        return yp, ys, (sp, ss)

    xp, xs, st0 = conv(xp, xs, cache_conv_l0, norm_mix_l0, conv_w_in_l0, conv_w_dw_l0, conv_norm_l0, conv_w_out_l0)
    xp, xs = ffn(xp, xs, norm_ffn_l0, ffn_w_up_l0, ffn_w_down_l0)
    xp, xs, st1 = _fox_layer(xp, xs, dims, cache_k_l1, cache_v_l1, cache_logf_l1, norm_mix_l1, fox_w_qkv_l1,
                             fox_w_f_l1, fox_b_f_l1, fox_q_norm_l1, fox_k_norm_l1, fox_w_o_l1)
    xp, xs = ffn(xp, xs, norm_ffn_l1, ffn_w_up_l1, ffn_w_down_l1)
    xp, xs, st2 = _gla_layer(xp, xs, dims, state_gla_l2, norm_mix_l2, gla_w_qkvr_l2, gla_w_a1_l2, gla_w_a2_l2,
                             gla_b_a_l2, gla_o_norm_l2, gla_w_o_l2)
    xp, xs = ffn(xp, xs, norm_ffn_l2, ffn_w_up_l2, ffn_w_down_l2)
    xp, xs, st3 = conv(xp, xs, cache_conv_l3, norm_mix_l3, conv_w_in_l3, conv_w_dw_l3, conv_norm_l3, conv_w_out_l3)
    xp, xs = ffn(xp, xs, norm_ffn_l3, ffn_w_up_l3, ffn_w_down_l3)
    return (xp.reshape(nb, nt, d), xs.reshape(nbs, nts, d), *st0, *st1, *st2, *st3)
```
